```python
import math
import jax, jax.numpy as jnp
from jax import lax
import numpy as np

D_MODEL = 1024
BATCH = 8
SEQ = 2048
DEPTH = 1

CHUNK = 64
Q_BLOCK = 128
HEAD_DIM = 64
V_DIM = 2 * HEAD_DIM
N_HEADS = D_MODEL // V_DIM
QK_WIDTH = N_HEADS * 2 * HEAD_DIM
ATTN_WIDTH = N_HEADS * V_DIM
CONV_WIDTH = D_MODEL
CONV_K = 3
ROPE_THETA = 10000.0
LN_EPS = 1e-5
RMS_EPS = 1e-5
DN_ALPHA = (2.0 * DEPTH) ** 0.25
DN_BETA = (8.0 * DEPTH) ** -0.25

SPLIT_WIDTHS = (QK_WIDTH, QK_WIDTH, ATTN_WIDTH, ATTN_WIDTH,
                CONV_WIDTH, CONV_WIDTH, CONV_WIDTH, CONV_WIDTH, 2 * D_MODEL)
IN_WIDTH = sum(SPLIT_WIDTHS)
SPLIT_POINTS = tuple(int(v) for v in np.cumsum(SPLIT_WIDTHS)[:-1])

kernel_name = "hybrid_diffattn_shortconv_gated_deepnorm"


def layer_norm(x, g, b):
    xf = x.astype(jnp.float32)
    mu = jnp.mean(xf, axis=-1, keepdims=True)
    var = jnp.mean(jnp.square(xf - mu), axis=-1, keepdims=True)
    y = (xf - mu) * lax.rsqrt(var + LN_EPS)
    return (y * g.astype(jnp.float32) + b.astype(jnp.float32)).astype(x.dtype)


def rms_norm(x, g):
    xf = x.astype(jnp.float32)
    y = xf * lax.rsqrt(jnp.mean(jnp.square(xf), axis=-1, keepdims=True) + RMS_EPS)
    return (y * g.astype(jnp.float32)).astype(x.dtype)


def rotary(t, seq_len):
    half = HEAD_DIM // 2
    inv_freq = 1.0 / (ROPE_THETA ** (jnp.arange(half, dtype=jnp.float32) / half))
    pos = jnp.arange(seq_len, dtype=jnp.float32)
    ang = pos[:, None] * inv_freq[None, :]
    cos = jnp.concatenate([jnp.cos(ang), jnp.cos(ang)], -1)[None, :, None, None, :]
    sin = jnp.concatenate([jnp.sin(ang), jnp.sin(ang)], -1)[None, :, None, None, :]
    t1, t2 = t[..., :half], t[..., half:]
    rot = jnp.concatenate([-t2, t1], axis=-1)
    return (t * cos.astype(t.dtype) + rot * sin.astype(t.dtype))


def diff_attention(q, k, v, lam):
    seq_len = q.shape[1]
    scale = HEAD_DIM ** -0.5
    outs = []
    for i in range(seq_len // Q_BLOCK):
        start, end = i * Q_BLOCK, (i + 1) * Q_BLOCK
        qb = q[:, start:end].astype(jnp.float32) * scale
        kb = k[:, :end].astype(jnp.float32)
        vb = v[:, :end].astype(jnp.float32)
        s = jnp.einsum('bqhcd,bkhcd->bhcqk', qb, kb)
        qpos = start + jnp.arange(Q_BLOCK)
        kpos = jnp.arange(end)
        allowed = (kpos[None, :] // CHUNK) <= (qpos[:, None] // CHUNK)
        s = jnp.where(allowed[None, None, None], s, -jnp.inf)
        p = jax.nn.softmax(s, axis=-1)
        a = p[:, :, 0] - lam * p[:, :, 1]
        outs.append(jnp.einsum('bhqk,bkhe->bqhe', a, vb).astype(v.dtype))
    return jnp.concatenate(outs, axis=1)


def causal_dwconv(u, w, b):
    rhs = w[:, None, :].astype(u.dtype)
    y = lax.conv_general_dilated(u, rhs, window_strides=(1,),
                                 padding=[(CONV_K - 1, 0)],
                                 dimension_numbers=('NWC', 'WIO', 'NWC'),
                                 feature_group_count=u.shape[-1])
    return y + b.astype(u.dtype)


def setup_inputs(seed: int = 0) -> dict:
    key = jax.random.key(seed)
    ks = jax.random.split(key, 16)
    f32 = jnp.float32
    x = jax.random.normal(ks[0], (BATCH, SEQ, D_MODEL), f32)
    w_in = jax.random.normal(ks[1], (DEPTH, D_MODEL, IN_WIDTH), f32) * D_MODEL ** -0.5
    v_lo = 2 * QK_WIDTH
    col_scale = jnp.ones((IN_WIDTH,), f32).at[v_lo:v_lo + ATTN_WIDTH].set(DN_BETA)
    w_in = w_in * col_scale
    b_gate = 0.01 * jax.random.normal(ks[2], (DEPTH, 2 * D_MODEL), f32)
    lambda_q1 = 0.1 * jax.random.normal(ks[3], (DEPTH, HEAD_DIM), f32)
    lambda_k1 = 0.1 * jax.random.normal(ks[4], (DEPTH, HEAD_DIM), f32)
    lambda_q2 = 0.1 * jax.random.normal(ks[5], (DEPTH, HEAD_DIM), f32)
    lambda_k2 = 0.1 * jax.random.normal(ks[6], (DEPTH, HEAD_DIM), f32)
    subln_g = 1.0 + 0.02 * jax.random.normal(ks[7], (DEPTH, V_DIM), f32)
    conv_w = jax.random.normal(ks[8], (DEPTH, CONV_K, CONV_WIDTH), f32) * CONV_K ** -0.5
    conv_b = 0.01 * jax.random.normal(ks[9], (DEPTH, CONV_WIDTH), f32)
    w_a_out = jax.random.normal(ks[10], (DEPTH, ATTN_WIDTH, D_MODEL), f32) * ATTN_WIDTH ** -0.5 * DN_BETA
    w_b_out = jax.random.normal(ks[11], (DEPTH, CONV_WIDTH, D_MODEL), f32) * CONV_WIDTH ** -0.5 * DN_BETA
    w_o = jax.random.normal(ks[12], (DEPTH, D_MODEL, D_MODEL), f32) * D_MODEL ** -0.5 * DN_BETA
    ln_g = 1.0 + 0.02 * jax.random.normal(ks[13], (DEPTH, D_MODEL), f32)
    ln_b = 0.01 * jax.random.normal(ks[14], (DEPTH, D_MODEL), f32)
    return {"x": x, "w_in": w_in, "b_gate": b_gate,
            "lambda_q1": lambda_q1, "lambda_k1": lambda_k1,
            "lambda_q2": lambda_q2, "lambda_k2": lambda_k2,
            "subln_g": subln_g, "conv_w": conv_w, "conv_b": conv_b,
            "w_a_out": w_a_out, "w_b_out": w_b_out, "w_o": w_o,
            "ln_g": ln_g, "ln_b": ln_b}


def reference(x, w_in, b_gate, lambda_q1, lambda_k1, lambda_q2, lambda_k2,
              subln_g, conv_w, conv_b, w_a_out, w_b_out, w_o, ln_g, ln_b):
    bsz, seq_len, _ = x.shape
    for l in range(DEPTH):
        lam_init = 0.8 - 0.6 * math.exp(-0.3 * l)
        proj = jnp.einsum('bsd,de->bse', x, w_in[l])
        q, k, v, z_a, h, bg, cg, z_b, gl = jnp.split(proj, SPLIT_POINTS, axis=-1)

        q = rotary(q.reshape(bsz, seq_len, N_HEADS, 2, HEAD_DIM), seq_len)
        k = rotary(k.reshape(bsz, seq_len, N_HEADS, 2, HEAD_DIM), seq_len)
        v = v.reshape(bsz, seq_len, N_HEADS, V_DIM)
        lam = (jnp.exp(jnp.sum(lambda_q1[l].astype(jnp.float32) * lambda_k1[l].astype(jnp.float32)))
               - jnp.exp(jnp.sum(lambda_q2[l].astype(jnp.float32) * lambda_k2[l].astype(jnp.float32)))
               + lam_init)
        o = diff_attention(q, k, v, lam)
        o = rms_norm(o, subln_g[l]) * (1.0 - lam_init)
        o = o.reshape(bsz, seq_len, ATTN_WIDTH) * jax.nn.silu(z_a)
        y_a = jnp.einsum('bse,ed->bsd', o, w_a_out[l])

        c = causal_dwconv(cg * h, conv_w[l], conv_b[l])
        c = bg * c * jax.nn.silu(z_b)
        y_b = jnp.einsum('bse,ed->bsd', c, w_b_out[l])

        g = jax.nn.sigmoid(gl + b_gate[l])
        g_a, g_b = g[..., :D_MODEL], g[..., D_MODEL:]
        merged = g_a * y_a + g_b * y_b
        out = jnp.einsum('bsd,de->bse', merged, w_o[l])

        x = layer_norm(DN_ALPHA * x + out, ln_g[l], ln_b[l])
    return x
```

```python
import functools
import math

import jax
import jax.numpy as jnp
from jax import lax
from jax.experimental import pallas as pl
from jax.experimental.pallas import tpu as pltpu

CHUNK = 64
HEAD_DIM = 64
V_DIM = 2 * HEAD_DIM
ROPE_THETA = 10000.0
LN_EPS = 1e-5
RMS_EPS = 1e-5
CONV_K = 3

V7X_LANES = 128
V7X_SUBLANES = 8
V7X_VMEM_LIMIT_BYTES = 56 * 1024 * 1024

PROJ_ROWS = 512
PROJ_COLS = 512
ATTN_ROWS = 256
OUT_ROWS = 512

_NT_DIMS = (((1,), (1,)), ((), ()))


def _sigmoid(z):
    return 1.0 / (1.0 + jnp.exp(-z))


def _proj_kernel(x_ref, w_ref, cos_ref, sin_ref, bg_ref, cw_ref, cb_ref,
                 q_ref, k_ref, v_ref, sa_ref, c_ref, g_ref, ubuf, *, d_model):
    tm = x_ref.shape[0]
    d = d_model
    cw = PROJ_COLS
    xb = x_ref[...].astype(jnp.bfloat16)

    def proj(lo):
        return jnp.dot(xb, w_ref[:, lo:lo + cw], preferred_element_type=jnp.float32)

    cos = cos_ref[...]
    sin_signed = sin_ref[...]
    lane = lax.broadcasted_iota(jnp.int32, (tm, V7X_LANES), 1)
    first_half = (lane & (HEAD_DIM - 1)) < (HEAD_DIM // 2)

    def rotary(t):
        partner = jnp.where(first_half,
                            pltpu.roll(t, V7X_LANES - HEAD_DIM // 2, 1),
                            pltpu.roll(t, HEAD_DIM // 2, 1))
        return t * cos + partner * sin_signed

    scale = HEAD_DIM ** -0.5
    for lo in range(0, d, cw):
        tq = proj(lo)
        tk = proj(d + lo)
        for j in range(0, cw, V7X_LANES):
            sl = slice(lo + j, lo + j + V7X_LANES)
            q_ref[:, sl] = (rotary(tq[:, j:j + V7X_LANES]) * scale).astype(q_ref.dtype)
            k_ref[:, sl] = rotary(tk[:, j:j + V7X_LANES]).astype(k_ref.dtype)

    for lo in range(0, d, cw):
        v_ref[:, lo:lo + cw] = proj(2 * d + lo).astype(v_ref.dtype)
        za = proj(3 * d + lo)
        sa_ref[:, lo:lo + cw] = (za * _sigmoid(za)).astype(sa_ref.dtype)

    @pl.when(pl.program_id(1) == 0)
    def _():
        ubuf[0:V7X_SUBLANES, :] = jnp.zeros((V7X_SUBLANES, d), jnp.float32)

    for lo in range(0, d, cw):
        sl = slice(lo, lo + cw)
        h = proj(4 * d + lo)
        bgate = proj(5 * d + lo)
        cgate = proj(6 * d + lo)
        zb = proj(7 * d + lo)
        u = cgate * h
        ubuf[V7X_SUBLANES:V7X_SUBLANES + tm, sl] = u
        u1 = ubuf[V7X_SUBLANES - 1:V7X_SUBLANES - 1 + tm, sl]
        u2 = ubuf[V7X_SUBLANES - 2:V7X_SUBLANES - 2 + tm, sl]
        conv = (cw_ref[0:1, sl] * u2 + cw_ref[1:2, sl] * u1
                + cw_ref[2:3, sl] * u + cb_ref[:, sl])
        c_ref[:, sl] = (bgate * conv * (zb * _sigmoid(zb))).astype(c_ref.dtype)
        ubuf[0:V7X_SUBLANES, sl] = ubuf[tm:tm + V7X_SUBLANES, sl]

    for lo in range(0, 2 * d, cw):
        gl = proj(8 * d + lo)
        g_ref[:, lo:lo + cw] = _sigmoid(gl + bg_ref[:, lo:lo + cw]).astype(g_ref.dtype)


def _proj_call(x, w_bf, cos, sin_signed, b_gate, conv_w, conv_b):
    bsz, seq, d = x.shape
    tm = PROJ_ROWS
    act = jax.ShapeDtypeStruct((bsz, seq, d), jnp.bfloat16)
    row_spec = pl.BlockSpec((None, tm, d), lambda b, i: (b, i, 0))
    tab_spec = pl.BlockSpec((tm, V7X_LANES), lambda b, i: (i, 0))

    def full(shape):
        return pl.BlockSpec(shape, lambda b, i: (0,) * len(shape))

    return pl.pallas_call(
        functools.partial(_proj_kernel, d_model=d),
        grid=(bsz, seq // tm),
        in_specs=[
            row_spec,
            pl.BlockSpec(w_bf.shape, lambda b, i: (0, 0), pipeline_mode=pl.Buffered(1)),
            tab_spec, tab_spec,
            full(b_gate.shape), full(conv_w.shape), full(conv_b.shape),
        ],
        out_specs=[row_spec, row_spec, row_spec, row_spec, row_spec,
                   pl.BlockSpec((None, tm, 2 * d), lambda b, i: (b, i, 0))],
        out_shape=[act, act, act, act, act,
                   jax.ShapeDtypeStruct((bsz, seq, 2 * d), jnp.bfloat16)],
        scratch_shapes=[pltpu.VMEM((tm + V7X_SUBLANES, d), jnp.float32)],
        compiler_params=pltpu.CompilerParams(
            dimension_semantics=("arbitrary", "arbitrary"),
            vmem_limit_bytes=V7X_VMEM_LIMIT_BYTES),
        name="proj_rotary_conv_gates",
    )(x, w_bf, cos, sin_signed, b_gate, conv_w, conv_b)


def _attn_kernel(lq1_ref, lk1_ref, lq2_ref, lk2_ref, gain_ref,
                 q_ref, k_ref, v_ref, sa_ref, o_ref, *, lam_init):
    seq = q_ref.shape[0]
    tq = ATTN_ROWS
    lam = (jnp.exp(jnp.sum(lq1_ref[...] * lk1_ref[...], axis=-1, keepdims=True))
           - jnp.exp(jnp.sum(lq2_ref[...] * lk2_ref[...], axis=-1, keepdims=True))
           + lam_init)

    lane = lax.broadcasted_iota(jnp.int32, (tq, V_DIM), 1)
    map1 = lane < HEAD_DIM
    row = lax.broadcasted_iota(jnp.int32, (2 * tq, tq), 0)
    col = lax.broadcasted_iota(jnp.int32, (2 * tq, tq), 1)
    qpos = jnp.where(row >= tq, row - tq, row)
    allowed = (col // CHUNK) <= (qpos // CHUNK)
    gain = gain_ref[...] * (1.0 - lam_init)

    for i in range(seq // tq):
        rows = slice(i * tq, (i + 1) * tq)
        q = q_ref[rows, :]
        zero = jnp.zeros_like(q)
        q2 = jnp.concatenate([jnp.where(map1, q, zero), jnp.where(map1, zero, q)], axis=0)

        sd = lax.dot_general(q2, k_ref[rows, :], _NT_DIMS,
                             preferred_element_type=jnp.float32)
        sd = jnp.where(allowed, sd, -jnp.inf)
        m = jnp.max(sd, axis=-1, keepdims=True)
        if i > 0:
            so = lax.dot_general(q2, k_ref[0:i * tq, :], _NT_DIMS,
                                 preferred_element_type=jnp.float32)
            m = jnp.maximum(m, jnp.max(so, axis=-1, keepdims=True))
        ed = jnp.exp(sd - m)
        l = jnp.sum(ed, axis=-1, keepdims=True)
        if i > 0:
            eo = jnp.exp(so - m)
            l = l + jnp.sum(eo, axis=-1, keepdims=True)
        l1 = l[:tq]
        ratio = lam * l1 / l[tq:]
        ad = (ed[:tq] - ratio * ed[tq:]).astype(jnp.bfloat16)
        acc = jnp.dot(ad, v_ref[rows, :], preferred_element_type=jnp.float32)
        if i > 0:
            ao = (eo[:tq] - ratio * eo[tq:]).astype(jnp.bfloat16)
            acc = acc + jnp.dot(ao, v_ref[0:i * tq, :], preferred_element_type=jnp.float32)
        o = acc / l1

        ms = jnp.mean(o * o, axis=-1, keepdims=True)
        y = o * lax.rsqrt(ms + RMS_EPS) * gain
        o_ref[rows, :] = (y * sa_ref[rows, :].astype(jnp.float32)).astype(o_ref.dtype)


def _attn_call(lq1, lk1, lq2, lk2, gain, q, k, v, sa, lam_init):
    bsz, seq, width = q.shape
    n_heads = width // V_DIM
    head_spec = pl.BlockSpec((None, seq, V_DIM), lambda b, h: (b, 0, h))

    def full(shape):
        return pl.BlockSpec(shape, lambda b, h: (0,) * len(shape))

    return pl.pallas_call(
        functools.partial(_attn_kernel, lam_init=lam_init),
        grid=(bsz, n_heads),
        in_specs=[full(lq1.shape), full(lk1.shape), full(lq2.shape), full(lk2.shape),
                  full(gain.shape), head_spec, head_spec, head_spec, head_spec],
        out_specs=head_spec,
        out_shape=jax.ShapeDtypeStruct((bsz, seq, width), jnp.bfloat16),
        compiler_params=pltpu.CompilerParams(
            dimension_semantics=("arbitrary", "arbitrary"),
            vmem_limit_bytes=V7X_VMEM_LIMIT_BYTES),
        name="diff_attention",
    )(lq1, lk1, lq2, lk2, gain, q, k, v, sa)


def _out_kernel(oa_ref, c_ref, g_ref, x_ref, wa_ref, wb_ref, wo_ref,
                lng_ref, lnb_ref, out_ref, *, alpha):
    d = x_ref.shape[1]
    ya = jnp.dot(oa_ref[...], wa_ref[...], preferred_element_type=jnp.float32)
    yb = jnp.dot(c_ref[...], wb_ref[...], preferred_element_type=jnp.float32)
    merged = (g_ref[:, 0:d].astype(jnp.float32) * ya
              + g_ref[:, d:2 * d].astype(jnp.float32) * yb)
    out = jnp.dot(merged.astype(jnp.bfloat16), wo_ref[...],
                  preferred_element_type=jnp.float32)
    r = alpha * x_ref[...] + out
    mu = jnp.mean(r, axis=-1, keepdims=True)
    dev = r - mu
    var = jnp.mean(dev * dev, axis=-1, keepdims=True)
    out_ref[...] = dev * lax.rsqrt(var + LN_EPS) * lng_ref[...] + lnb_ref[...]


def _out_call(oa, c, g, x, wa, wb, wo, ln_g, ln_b, alpha):
    bsz, seq, d = x.shape
    tm = OUT_ROWS
    row_spec = pl.BlockSpec((None, tm, d), lambda b, i: (b, i, 0))

    def full(shape):
        return pl.BlockSpec(shape, lambda b, i: (0,) * len(shape))

    return pl.pallas_call(
        functools.partial(_out_kernel, alpha=alpha),
        grid=(bsz, seq // tm),
        in_specs=[row_spec, row_spec,
                  pl.BlockSpec((None, tm, 2 * d), lambda b, i: (b, i, 0)),
                  row_spec, full(wa.shape), full(wb.shape), full(wo.shape),
                  full(ln_g.shape), full(ln_b.shape)],
        out_specs=row_spec,
        out_shape=jax.ShapeDtypeStruct((bsz, seq, d), jnp.float32),
        compiler_params=pltpu.CompilerParams(
            dimension_semantics=("arbitrary", "arbitrary"),
            vmem_limit_bytes=V7X_VMEM_LIMIT_BYTES),
        name="merge_out_layernorm",
    )(oa, c, g, x, wa, wb, wo, ln_g, ln_b)


def _rotary_tables(seq_len):
    half = HEAD_DIM // 2
    inv_freq = 1.0 / (ROPE_THETA ** (jnp.arange(half, dtype=jnp.float32) / half))
    pos = jnp.arange(seq_len, dtype=jnp.float32)
    ang = pos[:, None] * inv_freq[None, :]
    cos = jnp.cos(ang)
    sin = jnp.sin(ang)
    reps = V7X_LANES // HEAD_DIM
    cos_t = jnp.tile(jnp.concatenate([cos, cos], -1), (1, reps))
    sin_t = jnp.tile(jnp.concatenate([-sin, sin], -1), (1, reps))
    return cos_t, sin_t


def kernel(x, w_in, b_gate, lambda_q1, lambda_k1, lambda_q2, lambda_k2, subln_g,
           conv_w, conv_b, w_a_out, w_b_out, w_o, ln_g, ln_b):
    depth = w_in.shape[0]
    seq_len = x.shape[1]
    alpha = (2.0 * depth) ** 0.25
    cos_t, sin_t = _rotary_tables(seq_len)
    bf16 = jnp.bfloat16
    for l in range(depth):
        lam_init = 0.8 - 0.6 * math.exp(-0.3 * l)
        q, k, v, sa, c, g = _proj_call(
            x, w_in[l].astype(bf16), cos_t, sin_t,
            b_gate[l][None, :], conv_w[l], conv_b[l][None, :])
        oa = _attn_call(lambda_q1[l][None, :], lambda_k1[l][None, :],
                        lambda_q2[l][None, :], lambda_k2[l][None, :],
                        subln_g[l][None, :], q, k, v, sa, lam_init)
        x = _out_call(oa, c, g, x, w_a_out[l].astype(bf16), w_b_out[l].astype(bf16),
                      w_o[l].astype(bf16), ln_g[l][None, :], ln_b[l][None, :], alpha)
    return x
```

```python
import functools
import math

import jax
import jax.numpy as jnp
from jax import lax
from jax.experimental import pallas as pl
from jax.experimental.pallas import tpu as pltpu

CHUNK = 64
HEAD_DIM = 64
V_DIM = 2 * HEAD_DIM
ROPE_THETA = 10000.0
LN_EPS = 1e-5
RMS_EPS = 1e-5
CONV_K = 3

V7X_LANES = 128
V7X_SUBLANES = 8
V7X_VMEM_LIMIT_BYTES = 56 * 1024 * 1024

PROJ_ROWS = 512
PROJ_COLS = 512
ATTN_ROWS = 256
OUT_ROWS = 512

_NT_DIMS = (((1,), (1,)), ((), ()))


def _sigmoid(z):
    return 1.0 / (1.0 + jnp.exp(-z))


def _proj_kernel(x_ref, w_ref, cos_ref, sin_ref, bg_ref, cw_ref, cb_ref,
                 q_ref, k_ref, v_ref, sa_ref, c_ref, g_ref, ubuf, *, d_model):
    tm = x_ref.shape[0]
    d = d_model
    cw = PROJ_COLS
    xb = x_ref[...].astype(jnp.bfloat16)

    def proj(lo):
        return jnp.dot(xb, w_ref[:, lo:lo + cw], preferred_element_type=jnp.float32)

    cos = cos_ref[...]
    sin_signed = sin_ref[...]
    lane = lax.broadcasted_iota(jnp.int32, (tm, V7X_LANES), 1)
    first_half = (lane & (HEAD_DIM - 1)) < (HEAD_DIM // 2)

    def rotary(t):
        partner = jnp.where(first_half,
                            pltpu.roll(t, V7X_LANES - HEAD_DIM // 2, 1),
                            pltpu.roll(t, HEAD_DIM // 2, 1))
        return t * cos + partner * sin_signed

    scale = HEAD_DIM ** -0.5 * math.log2(math.e)
    for lo in range(0, d, cw):
        tq = proj(lo)
        tk = proj(d + lo)
        for j in range(0, cw, V7X_LANES):
            sl = slice(lo + j, lo + j + V7X_LANES)
            q_ref[:, sl] = (rotary(tq[:, j:j + V7X_LANES]) * scale).astype(q_ref.dtype)
            k_ref[:, sl] = rotary(tk[:, j:j + V7X_LANES]).astype(k_ref.dtype)

    for lo in range(0, d, cw):
        v_ref[:, lo:lo + cw] = proj(2 * d + lo).astype(v_ref.dtype)
        za = proj(3 * d + lo)
        sa_ref[:, lo:lo + cw] = (za * _sigmoid(za)).astype(sa_ref.dtype)

    @pl.when(pl.program_id(1) == 0)
    def _():
        ubuf[0:V7X_SUBLANES, :] = jnp.zeros((V7X_SUBLANES, d), jnp.float32)

    for lo in range(0, d, cw):
        sl = slice(lo, lo + cw)
        h = proj(4 * d + lo)
        bgate = proj(5 * d + lo)
        cgate = proj(6 * d + lo)
        zb = proj(7 * d + lo)
        u = cgate * h
        ubuf[V7X_SUBLANES:V7X_SUBLANES + tm, sl] = u
        u1 = ubuf[V7X_SUBLANES - 1:V7X_SUBLANES - 1 + tm, sl]
        u2 = ubuf[V7X_SUBLANES - 2:V7X_SUBLANES - 2 + tm, sl]
        conv = (cw_ref[0:1, sl] * u2 + cw_ref[1:2, sl] * u1
                + cw_ref[2:3, sl] * u + cb_ref[:, sl])
        c_ref[:, sl] = (bgate * conv * (zb * _sigmoid(zb))).astype(c_ref.dtype)
        ubuf[0:V7X_SUBLANES, sl] = ubuf[tm:tm + V7X_SUBLANES, sl]

    for lo in range(0, 2 * d, cw):
        gl = proj(8 * d + lo)
        g_ref[:, lo:lo + cw] = _sigmoid(gl + bg_ref[:, lo:lo + cw]).astype(g_ref.dtype)


def _proj_call(x, w_bf, cos, sin_signed, b_gate, conv_w, conv_b):
    bsz, seq, d = x.shape
    tm = PROJ_ROWS
    act = jax.ShapeDtypeStruct((bsz, seq, d), jnp.bfloat16)
    row_spec = pl.BlockSpec((None, tm, d), lambda b, i: (b, i, 0))
    tab_spec = pl.BlockSpec((tm, V7X_LANES), lambda b, i: (i, 0))

    def full(shape):
        return pl.BlockSpec(shape, lambda b, i: (0,) * len(shape))

    return pl.pallas_call(
        functools.partial(_proj_kernel, d_model=d),
        grid=(bsz, seq // tm),
        in_specs=[
            row_spec,
            pl.BlockSpec(w_bf.shape, lambda b, i: (0, 0), pipeline_mode=pl.Buffered(1)),
            tab_spec, tab_spec,
            full(b_gate.shape), full(conv_w.shape), full(conv_b.shape),
        ],
        out_specs=[row_spec, row_spec, row_spec, row_spec, row_spec,
                   pl.BlockSpec((None, tm, 2 * d), lambda b, i: (b, i, 0))],
        out_shape=[act, act, act, act, act,
                   jax.ShapeDtypeStruct((bsz, seq, 2 * d), jnp.bfloat16)],
        scratch_shapes=[pltpu.VMEM((tm + V7X_SUBLANES, d), jnp.float32)],
        compiler_params=pltpu.CompilerParams(
            dimension_semantics=("arbitrary", "arbitrary"),
            vmem_limit_bytes=V7X_VMEM_LIMIT_BYTES),
        name="proj_rotary_conv_gates",
    )(x, w_bf, cos, sin_signed, b_gate, conv_w, conv_b)


def _attn_kernel(lq1_ref, lk1_ref, lq2_ref, lk2_ref, gain_ref,
                 q_ref, k_ref, v_ref, sa_ref, o_ref, *, lam_init):
    seq = q_ref.shape[0]
    tq = ATTN_ROWS
    lam = (jnp.exp(jnp.sum(lq1_ref[...] * lk1_ref[...], axis=-1, keepdims=True))
           - jnp.exp(jnp.sum(lq2_ref[...] * lk2_ref[...], axis=-1, keepdims=True))
           + lam_init)

    lane = lax.broadcasted_iota(jnp.int32, (tq, V_DIM), 1)
    map1 = lane < HEAD_DIM
    row = lax.broadcasted_iota(jnp.int32, (2 * tq, tq), 0)
    col = lax.broadcasted_iota(jnp.int32, (2 * tq, tq), 1)
    qpos = jnp.where(row >= tq, row - tq, row)
    allowed = (col // CHUNK) <= (qpos // CHUNK)
    gain = gain_ref[...] * (1.0 - lam_init)

    def scores(i):
        rows = slice(i * tq, (i + 1) * tq)
        q = q_ref[rows, :]
        zero = jnp.zeros_like(q)
        q2 = jnp.concatenate([jnp.where(map1, q, zero), jnp.where(map1, zero, q)], axis=0)
        sd = lax.dot_general(q2, k_ref[rows, :], _NT_DIMS,
                             preferred_element_type=jnp.float32)
        sd = jnp.where(allowed, sd, -jnp.inf)
        so = None
        if i > 0:
            so = lax.dot_general(q2, k_ref[0:i * tq, :], _NT_DIMS,
                                 preferred_element_type=jnp.float32)
        return sd, so

    def finish(i, sd, so):
        rows = slice(i * tq, (i + 1) * tq)
        m = jnp.max(sd, axis=-1, keepdims=True)
        if i > 0:
            m = jnp.maximum(m, jnp.max(so, axis=-1, keepdims=True))
        ed = jnp.exp2(sd - m)
        l = jnp.sum(ed, axis=-1, keepdims=True)
        if i > 0:
            eo = jnp.exp2(so - m)
            l = l + jnp.sum(eo, axis=-1, keepdims=True)
        l1 = l[:tq]
        ratio = lam * l1 / l[tq:]
        ad = (ed[:tq] - ratio * ed[tq:]).astype(jnp.bfloat16)
        acc = jnp.dot(ad, v_ref[rows, :], preferred_element_type=jnp.float32)
        if i > 0:
            ao = (eo[:tq] - ratio * eo[tq:]).astype(jnp.bfloat16)
            acc = acc + jnp.dot(ao, v_ref[0:i * tq, :], preferred_element_type=jnp.float32)
        o = acc / l1

        ms = jnp.mean(o * o, axis=-1, keepdims=True)
        y = o * lax.rsqrt(ms + RMS_EPS) * gain
        o_ref[rows, :] = (y * sa_ref[rows, :].astype(jnp.float32)).astype(o_ref.dtype)

    n_tiles = seq // tq
    nxt = scores(0)
    for i in range(n_tiles):
        cur = nxt
        if i + 1 < n_tiles:
            nxt = scores(i + 1)
        finish(i, *cur)


def _attn_call(lq1, lk1, lq2, lk2, gain, q, k, v, sa, lam_init):
    bsz, seq, width = q.shape
    n_heads = width // V_DIM
    head_spec = pl.BlockSpec((None, seq, V_DIM), lambda b, h: (b, 0, h))

    def full(shape):
        return pl.BlockSpec(shape, lambda b, h: (0,) * len(shape))

    return pl.pallas_call(
        functools.partial(_attn_kernel, lam_init=lam_init),
        grid=(bsz, n_heads),
        in_specs=[full(lq1.shape), full(lk1.shape), full(lq2.shape), full(lk2.shape),
                  full(gain.shape), head_spec, head_spec, head_spec, head_spec],
        out_specs=head_spec,
        out_shape=jax.ShapeDtypeStruct((bsz, seq, width), jnp.bfloat16),
        compiler_params=pltpu.CompilerParams(
            dimension_semantics=("arbitrary", "arbitrary"),
            vmem_limit_bytes=V7X_VMEM_LIMIT_BYTES),
        name="diff_attention",
    )(lq1, lk1, lq2, lk2, gain, q, k, v, sa)


def _out_kernel(oa_ref, c_ref, g_ref, x_ref, wa_ref, wb_ref, wo_ref,
                lng_ref, lnb_ref, out_ref, *, alpha):
    d = x_ref.shape[1]
    ya = jnp.dot(oa_ref[...], wa_ref[...], preferred_element_type=jnp.float32)
    yb = jnp.dot(c_ref[...], wb_ref[...], preferred_element_type=jnp.float32)
    merged = (g_ref[:, 0:d].astype(jnp.float32) * ya
              + g_ref[:, d:2 * d].astype(jnp.float32) * yb)
    out = jnp.dot(merged.astype(jnp.bfloat16), wo_ref[...],
                  preferred_element_type=jnp.float32)
    r = alpha * x_ref[...] + out
    mu = jnp.mean(r, axis=-1, keepdims=True)
    dev = r - mu
    var = jnp.mean(dev * dev, axis=-1, keepdims=True)
    out_ref[...] = dev * lax.rsqrt(var + LN_EPS) * lng_ref[...] + lnb_ref[...]


def _out_call(oa, c, g, x, wa, wb, wo, ln_g, ln_b, alpha):
    bsz, seq, d = x.shape
    tm = OUT_ROWS
    row_spec = pl.BlockSpec((None, tm, d), lambda b, i: (b, i, 0))

    def full(shape):
        return pl.BlockSpec(shape, lambda b, i: (0,) * len(shape))

    return pl.pallas_call(
        functools.partial(_out_kernel, alpha=alpha),
        grid=(bsz, seq // tm),
        in_specs=[row_spec, row_spec,
                  pl.BlockSpec((None, tm, 2 * d), lambda b, i: (b, i, 0)),
                  row_spec, full(wa.shape), full(wb.shape), full(wo.shape),
                  full(ln_g.shape), full(ln_b.shape)],
        out_specs=row_spec,
        out_shape=jax.ShapeDtypeStruct((bsz, seq, d), jnp.float32),
        compiler_params=pltpu.CompilerParams(
            dimension_semantics=("arbitrary", "arbitrary"),
            vmem_limit_bytes=V7X_VMEM_LIMIT_BYTES),
        name="merge_out_layernorm",
    )(oa, c, g, x, wa, wb, wo, ln_g, ln_b)


def _rotary_tables(seq_len):
    half = HEAD_DIM // 2
    inv_freq = 1.0 / (ROPE_THETA ** (jnp.arange(half, dtype=jnp.float32) / half))
    pos = jnp.arange(seq_len, dtype=jnp.float32)
    ang = pos[:, None] * inv_freq[None, :]
    cos = jnp.cos(ang)
    sin = jnp.sin(ang)
    reps = V7X_LANES // HEAD_DIM
    cos_t = jnp.tile(jnp.concatenate([cos, cos], -1), (1, reps))
    sin_t = jnp.tile(jnp.concatenate([-sin, sin], -1), (1, reps))
    return cos_t, sin_t


def kernel(x, w_in, b_gate, lambda_q1, lambda_k1, lambda_q2, lambda_k2, subln_g,
           conv_w, conv_b, w_a_out, w_b_out, w_o, ln_g, ln_b):
    depth = w_in.shape[0]
    seq_len = x.shape[1]
    alpha = (2.0 * depth) ** 0.25
    cos_t, sin_t = _rotary_tables(seq_len)
    bf16 = jnp.bfloat16
    for l in range(depth):
        lam_init = 0.8 - 0.6 * math.exp(-0.3 * l)
        q, k, v, sa, c, g = _proj_call(
            x, w_in[l].astype(bf16), cos_t, sin_t,
            b_gate[l][None, :], conv_w[l], conv_b[l][None, :])
        oa = _attn_call(lambda_q1[l][None, :], lambda_k1[l][None, :],
                        lambda_q2[l][None, :], lambda_k2[l][None, :],
                        subln_g[l][None, :], q, k, v, sa, lam_init)
        x = _out_call(oa, c, g, x, w_a_out[l].astype(bf16), w_b_out[l].astype(bf16),
                      w_o[l].astype(bf16), ln_g[l][None, :], ln_b[l][None, :], alpha)
    return x
```

```python
import functools
import math

import jax
import jax.numpy as jnp
from jax import lax
from jax.experimental import pallas as pl
from jax.experimental.pallas import tpu as pltpu

CHUNK = 64
HEAD_DIM = 64
V_DIM = 2 * HEAD_DIM
ROPE_THETA = 10000.0
LN_EPS = 1e-5
RMS_EPS = 1e-5
CONV_K = 3

V7X_LANES = 128
V7X_SUBLANES = 8
V7X_VMEM_LIMIT_BYTES = 56 * 1024 * 1024

PROJ_ROWS = 512
PROJ_COLS = 256
ATTN_ROWS = 256
OUT_ROWS = 1024
OUT_SUB_ROWS = 256
OUT_COLS = 256

_NT_DIMS = (((1,), (1,)), ((), ()))


def _sigmoid(z):
    return 1.0 / (1.0 + jnp.exp(-z))


def _run_pipelined(stages):
    pending = None
    for dots, epilogue in stages:
        acc = dots()
        if pending is not None:
            pending[0](*pending[1])
        pending = (epilogue, acc)
    pending[0](*pending[1])


def _proj_kernel(x_ref, w_ref, cos_ref, sin_ref, bg_ref, cw_ref, cb_ref,
                 q_ref, k_ref, v_ref, sa_ref, c_ref, g_ref, ubuf, *, d_model):
    tm = x_ref.shape[0]
    d = d_model
    cw = PROJ_COLS
    xb = x_ref[...].astype(jnp.bfloat16)

    def proj(lo):
        return jnp.dot(xb, w_ref[:, lo:lo + cw], preferred_element_type=jnp.float32)

    cos = cos_ref[...]
    sin_signed = sin_ref[...]
    lane = lax.broadcasted_iota(jnp.int32, (tm, V7X_LANES), 1)
    first_half = (lane & (HEAD_DIM - 1)) < (HEAD_DIM // 2)

    def rotary(t):
        partner = jnp.where(first_half,
                            pltpu.roll(t, V7X_LANES - HEAD_DIM // 2, 1),
                            pltpu.roll(t, HEAD_DIM // 2, 1))
        return t * cos + partner * sin_signed

    scale = HEAD_DIM ** -0.5 * math.log2(math.e)

    def rotary_epilogue(out_ref, lo, mult):
        def epilogue(t):
            for j in range(0, cw, V7X_LANES):
                r = rotary(t[:, j:j + V7X_LANES])
                if mult != 1.0:
                    r = r * mult
                out_ref[:, lo + j:lo + j + V7X_LANES] = r.astype(out_ref.dtype)
        return epilogue

    def v_epilogue(lo):
        def epilogue(t):
            v_ref[:, lo:lo + cw] = t.astype(v_ref.dtype)
        return epilogue

    def silu_epilogue(lo):
        def epilogue(za):
            sa_ref[:, lo:lo + cw] = (za * _sigmoid(za)).astype(sa_ref.dtype)
        return epilogue

    def conv_in_epilogue(lo):
        def epilogue(h, cgate):
            ubuf[V7X_SUBLANES:V7X_SUBLANES + tm, lo:lo + cw] = cgate * h
        return epilogue

    def conv_out_epilogue(lo):
        sl = slice(lo, lo + cw)

        def epilogue(bgate, zb):
            u = ubuf[V7X_SUBLANES:V7X_SUBLANES + tm, sl]
            u1 = ubuf[V7X_SUBLANES - 1:V7X_SUBLANES - 1 + tm, sl]
            u2 = ubuf[V7X_SUBLANES - 2:V7X_SUBLANES - 2 + tm, sl]
            conv = (cw_ref[0:1, sl] * u2 + cw_ref[1:2, sl] * u1
                    + cw_ref[2:3, sl] * u + cb_ref[:, sl])
            c_ref[:, sl] = (bgate * conv * (zb * _sigmoid(zb))).astype(c_ref.dtype)
            ubuf[0:V7X_SUBLANES, sl] = ubuf[tm:tm + V7X_SUBLANES, sl]
        return epilogue

    def gate_epilogue(lo):
        def epilogue(gl):
            g_ref[:, lo:lo + cw] = _sigmoid(gl + bg_ref[:, lo:lo + cw]).astype(g_ref.dtype)
        return epilogue

    stages = []
    for lo in range(0, d, cw):
        stages.append(((lo,), rotary_epilogue(q_ref, lo, scale)))
        stages.append(((d + lo,), rotary_epilogue(k_ref, lo, 1.0)))
    for lo in range(0, d, cw):
        stages.append(((3 * d + lo,), silu_epilogue(lo)))
    for lo in range(0, d, cw):
        stages.append(((4 * d + lo, 6 * d + lo), conv_in_epilogue(lo)))
        stages.append(((5 * d + lo, 7 * d + lo), conv_out_epilogue(lo)))
    for lo in range(0, 2 * d, cw):
        stages.append(((8 * d + lo,), gate_epilogue(lo)))
    for lo in range(0, d, cw):
        stages.append(((2 * d + lo,), v_epilogue(lo)))

    @pl.when(pl.program_id(1) == 0)
    def _():
        ubuf[0:V7X_SUBLANES, :] = jnp.zeros((V7X_SUBLANES, d), jnp.float32)

    _run_pipelined([(functools.partial(lambda cols: [proj(lo) for lo in cols], cols), epilogue)
                    for cols, epilogue in stages])


def _proj_call(x, w_bf, cos, sin_signed, b_gate, conv_w, conv_b):
    bsz, seq, d = x.shape
    tm = PROJ_ROWS
    act = jax.ShapeDtypeStruct((bsz, seq, d), jnp.bfloat16)
    row_spec = pl.BlockSpec((None, tm, d), lambda b, i: (b, i, 0))
    tab_spec = pl.BlockSpec((tm, V7X_LANES), lambda b, i: (i, 0))

    def full(shape):
        return pl.BlockSpec(shape, lambda b, i: (0,) * len(shape))

    return pl.pallas_call(
        functools.partial(_proj_kernel, d_model=d),
        grid=(bsz, seq // tm),
        in_specs=[
            row_spec,
            pl.BlockSpec(w_bf.shape, lambda b, i: (0, 0), pipeline_mode=pl.Buffered(1)),
            tab_spec, tab_spec,
            full(b_gate.shape), full(conv_w.shape), full(conv_b.shape),
        ],
        out_specs=[row_spec, row_spec, row_spec, row_spec, row_spec,
                   pl.BlockSpec((None, tm, 2 * d), lambda b, i: (b, i, 0))],
        out_shape=[act, act, act, act, act,
                   jax.ShapeDtypeStruct((bsz, seq, 2 * d), jnp.bfloat16)],
        scratch_shapes=[pltpu.VMEM((tm + V7X_SUBLANES, d), jnp.float32)],
        compiler_params=pltpu.CompilerParams(
            dimension_semantics=("arbitrary", "arbitrary"),
            vmem_limit_bytes=V7X_VMEM_LIMIT_BYTES),
        name="proj_rotary_conv_gates",
    )(x, w_bf, cos, sin_signed, b_gate, conv_w, conv_b)


def _attn_kernel(lq1_ref, lk1_ref, lq2_ref, lk2_ref, gain_ref,
                 q_ref, k_ref, v_ref, sa_ref, o_ref, *, lam_init):
    seq = q_ref.shape[0]
    tq = ATTN_ROWS
    lam = (jnp.exp(jnp.sum(lq1_ref[...] * lk1_ref[...], axis=-1, keepdims=True))
           - jnp.exp(jnp.sum(lq2_ref[...] * lk2_ref[...], axis=-1, keepdims=True))
           + lam_init)

    lane = lax.broadcasted_iota(jnp.int32, (tq, V_DIM), 1)
    map1 = lane < HEAD_DIM
    row = lax.broadcasted_iota(jnp.int32, (2 * tq, tq), 0)
    col = lax.broadcasted_iota(jnp.int32, (2 * tq, tq), 1)
    qpos = jnp.where(row >= tq, row - tq, row)
    allowed = (col // CHUNK) <= (qpos // CHUNK)
    gain = gain_ref[...] * (1.0 - lam_init)

    def scores(i):
        rows = slice(i * tq, (i + 1) * tq)
        q = q_ref[rows, :]
        zero = jnp.zeros_like(q)
        q2 = jnp.concatenate([jnp.where(map1, q, zero), jnp.where(map1, zero, q)], axis=0)
        sd = lax.dot_general(q2, k_ref[rows, :], _NT_DIMS,
                             preferred_element_type=jnp.float32)
        sd = jnp.where(allowed, sd, -jnp.inf)
        so = None
        if i > 0:
            so = lax.dot_general(q2, k_ref[0:i * tq, :], _NT_DIMS,
                                 preferred_element_type=jnp.float32)
        return sd, so

    def finish(i, sd, so):
        rows = slice(i * tq, (i + 1) * tq)
        m = jnp.max(sd, axis=-1, keepdims=True)
        if i > 0:
            m = jnp.maximum(m, jnp.max(so, axis=-1, keepdims=True))
        ed = jnp.exp2(sd - m)
        l = jnp.sum(ed, axis=-1, keepdims=True)
        if i > 0:
            eo = jnp.exp2(so - m)
            l = l + jnp.sum(eo, axis=-1, keepdims=True)
        l1 = l[:tq]
        ratio = lam * l1 / l[tq:]
        ad = (ed[:tq] - ratio * ed[tq:]).astype(jnp.bfloat16)
        acc = jnp.dot(ad, v_ref[rows, :], preferred_element_type=jnp.float32)
        if i > 0:
            ao = (eo[:tq] - ratio * eo[tq:]).astype(jnp.bfloat16)
            acc = acc + jnp.dot(ao, v_ref[0:i * tq, :], preferred_element_type=jnp.float32)
        o = acc / l1

        ms = jnp.mean(o * o, axis=-1, keepdims=True)
        y = o * lax.rsqrt(ms + RMS_EPS) * gain
        o_ref[rows, :] = (y * sa_ref[rows, :].astype(jnp.float32)).astype(o_ref.dtype)

    n_tiles = seq // tq
    nxt = scores(0)
    for i in range(n_tiles):
        cur = nxt
        if i + 1 < n_tiles:
            nxt = scores(i + 1)
        finish(i, *cur)


def _attn_call(lq1, lk1, lq2, lk2, gain, q, k, v, sa, lam_init):
    bsz, seq, width = q.shape
    n_heads = width // V_DIM
    head_spec = pl.BlockSpec((None, seq, V_DIM), lambda b, h: (b, 0, h))

    def full(shape):
        return pl.BlockSpec(shape, lambda b, h: (0,) * len(shape))

    return pl.pallas_call(
        functools.partial(_attn_kernel, lam_init=lam_init),
        grid=(bsz, n_heads),
        in_specs=[full(lq1.shape), full(lk1.shape), full(lq2.shape), full(lk2.shape),
                  full(gain.shape), head_spec, head_spec, head_spec, head_spec],
        out_specs=head_spec,
        out_shape=jax.ShapeDtypeStruct((bsz, seq, width), jnp.bfloat16),
        compiler_params=pltpu.CompilerParams(
            dimension_semantics=("arbitrary", "arbitrary"),
            vmem_limit_bytes=V7X_VMEM_LIMIT_BYTES),
        name="diff_attention",
    )(lq1, lk1, lq2, lk2, gain, q, k, v, sa)


def _out_kernel(oa_ref, c_ref, g_ref, x_ref, wa_ref, wb_ref, wo_ref,
                lng_ref, lnb_ref, out_ref, *, alpha):
    tm, d = x_ref.shape
    cw = OUT_COLS
    f32 = jnp.float32

    def merge_stage(rows, lo, merged):
        def dots():
            return [jnp.dot(oa_ref[rows, :], wa_ref[:, lo:lo + cw], preferred_element_type=f32),
                    jnp.dot(c_ref[rows, :], wb_ref[:, lo:lo + cw], preferred_element_type=f32)]

        def epilogue(ya, yb):
            merged.append((g_ref[rows, lo:lo + cw].astype(f32) * ya
                           + g_ref[rows, d + lo:d + lo + cw].astype(f32) * yb
                           ).astype(jnp.bfloat16))
        return dots, epilogue

    def out_stage(rows, lo, merged, resid):
        def dots():
            lhs = jnp.concatenate(merged, axis=1)
            return [jnp.dot(lhs, wo_ref[:, lo:lo + cw], preferred_element_type=f32)]

        def epilogue(out):
            resid.append(alpha * x_ref[rows, lo:lo + cw] + out)
            if lo + cw == d:
                mu = sum(jnp.sum(r, axis=-1, keepdims=True) for r in resid) / d
                dev = [r - mu for r in resid]
                var = sum(jnp.sum(v * v, axis=-1, keepdims=True) for v in dev) / d
                inv = lax.rsqrt(var + LN_EPS)
                for j, v in enumerate(dev):
                    cols = slice(j * cw, (j + 1) * cw)
                    out_ref[rows, cols] = v * inv * lng_ref[:, cols] + lnb_ref[:, cols]
        return dots, epilogue

    col_starts = range(0, d, cw)
    sub_tiles = []
    for r0 in range(0, tm, OUT_SUB_ROWS):
        rows = slice(r0, r0 + OUT_SUB_ROWS)
        merged, resid = [], []
        sub_tiles.append(([merge_stage(rows, lo, merged) for lo in col_starts],
                          [out_stage(rows, lo, merged, resid) for lo in col_starts]))
    assert len(sub_tiles) >= 2
    stages = list(sub_tiles[0][0])
    for prev, cur in zip(sub_tiles[:-1], sub_tiles[1:]):
        for m_stage, o_stage in zip(cur[0], prev[1]):
            stages += [m_stage, o_stage]
    stages += sub_tiles[-1][1]
    _run_pipelined(stages)


def _out_call(oa, c, g, x, wa, wb, wo, ln_g, ln_b, alpha):
    bsz, seq, d = x.shape
    tm = OUT_ROWS
    row_spec = pl.BlockSpec((None, tm, d), lambda b, i: (b, i, 0))

    def full(shape, **kwargs):
        return pl.BlockSpec(shape, lambda b, i: (0,) * len(shape), **kwargs)

    resident = dict(pipeline_mode=pl.Buffered(1))
    return pl.pallas_call(
        functools.partial(_out_kernel, alpha=alpha),
        grid=(bsz, seq // tm),
        in_specs=[row_spec, row_spec,
                  pl.BlockSpec((None, tm, 2 * d), lambda b, i: (b, i, 0)),
                  row_spec, full(wa.shape, **resident), full(wb.shape, **resident),
                  full(wo.shape, **resident), full(ln_g.shape), full(ln_b.shape)],
        out_specs=row_spec,
        out_shape=jax.ShapeDtypeStruct((bsz, seq, d), jnp.float32),
        compiler_params=pltpu.CompilerParams(
            dimension_semantics=("arbitrary", "arbitrary"),
            vmem_limit_bytes=V7X_VMEM_LIMIT_BYTES),
        name="merge_out_layernorm",
    )(oa, c, g, x, wa, wb, wo, ln_g, ln_b)


def _rotary_tables(seq_len):
    half = HEAD_DIM // 2
    inv_freq = 1.0 / (ROPE_THETA ** (jnp.arange(half, dtype=jnp.float32) / half))
    pos = jnp.arange(seq_len, dtype=jnp.float32)
    ang = pos[:, None] * inv_freq[None, :]
    cos = jnp.cos(ang)
    sin = jnp.sin(ang)
    reps = V7X_LANES // HEAD_DIM
    cos_t = jnp.tile(jnp.concatenate([cos, cos], -1), (1, reps))
    sin_t = jnp.tile(jnp.concatenate([-sin, sin], -1), (1, reps))
    return cos_t, sin_t


def kernel(x, w_in, b_gate, lambda_q1, lambda_k1, lambda_q2, lambda_k2, subln_g,
           conv_w, conv_b, w_a_out, w_b_out, w_o, ln_g, ln_b):
    depth = w_in.shape[0]
    seq_len = x.shape[1]
    alpha = (2.0 * depth) ** 0.25
    cos_t, sin_t = _rotary_tables(seq_len)
    bf16 = jnp.bfloat16
    for l in range(depth):
        lam_init = 0.8 - 0.6 * math.exp(-0.3 * l)
        q, k, v, sa, c, g = _proj_call(
            x, w_in[l].astype(bf16), cos_t, sin_t,
            b_gate[l][None, :], conv_w[l], conv_b[l][None, :])
        oa = _attn_call(lambda_q1[l][None, :], lambda_k1[l][None, :],
                        lambda_q2[l][None, :], lambda_k2[l][None, :],
                        subln_g[l][None, :], q, k, v, sa, lam_init)
        x = _out_call(oa, c, g, x, w_a_out[l].astype(bf16), w_b_out[l].astype(bf16),
                      w_o[l].astype(bf16), ln_g[l][None, :], ln_b[l][None, :], alpha)
    return x
```

```python
import functools
import itertools
import math

import jax
import jax.numpy as jnp
from jax import lax
from jax.experimental import pallas as pl
from jax.experimental.pallas import tpu as pltpu

CHUNK = 64
HEAD_DIM = 64
V_DIM = 2 * HEAD_DIM
ROPE_THETA = 10000.0
LN_EPS = 1e-5
RMS_EPS = 1e-5
CONV_K = 3

V7X_LANES = 128
V7X_SUBLANES = 8
V7X_VMEM_LIMIT_BYTES = 56 * 1024 * 1024

PROJ_ROWS = 512
PROJ_COLS = 256
ATTN_ROWS = 256
ATTN_KEYS = 512
OUT_ROWS = 1024
OUT_SUB_ROWS = 256
OUT_COLS = 256

_NT_DIMS = (((1,), (1,)), ((), ()))


def _sigmoid(z):
    return 1.0 / (1.0 + jnp.exp(-z))


def _run_pipelined(stages):
    pending = None
    for dots, epilogue in stages:
        acc = dots()
        if pending is not None:
            pending[0](*pending[1])
        pending = (epilogue, acc)
    pending[0](*pending[1])


def _proj_kernel(x_ref, w_ref, cos_ref, sin_ref, bg_ref, cw_ref, cb_ref,
                 q_ref, k_ref, vt_ref, sa_ref, c_ref, g_ref, ubuf, *, d_model):
    tm = x_ref.shape[0]
    d = d_model
    cw = PROJ_COLS
    xb = x_ref[...].astype(jnp.bfloat16)

    def proj(lo):
        return jnp.dot(xb, w_ref[:, lo:lo + cw], preferred_element_type=jnp.float32)

    cos = cos_ref[...]
    sin_signed = sin_ref[...]
    lane = lax.broadcasted_iota(jnp.int32, (tm, V7X_LANES), 1)
    first_half = (lane & (HEAD_DIM - 1)) < (HEAD_DIM // 2)

    def rotary(t):
        partner = jnp.where(first_half,
                            pltpu.roll(t, V7X_LANES - HEAD_DIM // 2, 1),
                            pltpu.roll(t, HEAD_DIM // 2, 1))
        return t * cos + partner * sin_signed

    scale = HEAD_DIM ** -0.5 * math.log2(math.e)

    def rotary_epilogue(out_ref, lo, mult):
        def epilogue(t):
            for j in range(0, cw, V7X_LANES):
                r = rotary(t[:, j:j + V7X_LANES])
                if mult != 1.0:
                    r = r * mult
                out_ref[:, lo + j:lo + j + V7X_LANES] = r.astype(out_ref.dtype)
        return epilogue

    def v_epilogue(lo):
        def epilogue(t):
            vt_ref[lo:lo + cw, :] = t.T.astype(vt_ref.dtype)
        return epilogue

    def silu_epilogue(lo):
        def epilogue(za):
            sa_ref[:, lo:lo + cw] = (za * _sigmoid(za)).astype(sa_ref.dtype)
        return epilogue

    def conv_in_epilogue(lo):
        def epilogue(h, cgate):
            ubuf[V7X_SUBLANES:V7X_SUBLANES + tm, lo:lo + cw] = cgate * h
        return epilogue

    def conv_out_epilogue(lo):
        sl = slice(lo, lo + cw)

        def epilogue(bgate, zb):
            u = ubuf[V7X_SUBLANES:V7X_SUBLANES + tm, sl]
            u1 = ubuf[V7X_SUBLANES - 1:V7X_SUBLANES - 1 + tm, sl]
            u2 = ubuf[V7X_SUBLANES - 2:V7X_SUBLANES - 2 + tm, sl]
            conv = (cw_ref[0:1, sl] * u2 + cw_ref[1:2, sl] * u1
                    + cw_ref[2:3, sl] * u + cb_ref[:, sl])
            c_ref[:, sl] = (bgate * conv * (zb * _sigmoid(zb))).astype(c_ref.dtype)
            ubuf[0:V7X_SUBLANES, sl] = ubuf[tm:tm + V7X_SUBLANES, sl]
        return epilogue

    def gate_epilogue(lo):
        def epilogue(gl):
            g_ref[:, lo:lo + cw] = _sigmoid(gl + bg_ref[:, lo:lo + cw]).astype(g_ref.dtype)
        return epilogue

    stages = []
    for lo in range(0, d, cw):
        stages.append(((lo,), rotary_epilogue(q_ref, lo, scale)))
        stages.append(((d + lo,), rotary_epilogue(k_ref, lo, 1.0)))
    for lo in range(0, d, cw):
        stages.append(((3 * d + lo,), silu_epilogue(lo)))
    for lo in range(0, d, cw):
        stages.append(((4 * d + lo, 6 * d + lo), conv_in_epilogue(lo)))
        stages.append(((5 * d + lo, 7 * d + lo), conv_out_epilogue(lo)))
    for lo in range(0, 2 * d, cw):
        stages.append(((8 * d + lo,), gate_epilogue(lo)))
    for lo in range(0, d, cw):
        stages.append(((2 * d + lo,), v_epilogue(lo)))

    @pl.when(pl.program_id(1) == 0)
    def _():
        ubuf[0:V7X_SUBLANES, :] = jnp.zeros((V7X_SUBLANES, d), jnp.float32)

    _run_pipelined([(functools.partial(lambda cols: [proj(lo) for lo in cols], cols), epilogue)
                    for cols, epilogue in stages])


def _proj_call(x, w_bf, cos, sin_signed, b_gate, conv_w, conv_b):
    bsz, seq, d = x.shape
    tm = PROJ_ROWS
    act = jax.ShapeDtypeStruct((bsz, seq, d), jnp.bfloat16)
    row_spec = pl.BlockSpec((None, tm, d), lambda b, i: (b, i, 0))
    tab_spec = pl.BlockSpec((tm, V7X_LANES), lambda b, i: (i, 0))

    def full(shape):
        return pl.BlockSpec(shape, lambda b, i: (0,) * len(shape))

    return pl.pallas_call(
        functools.partial(_proj_kernel, d_model=d),
        grid=(bsz, seq // tm),
        in_specs=[
            row_spec,
            pl.BlockSpec(w_bf.shape, lambda b, i: (0, 0), pipeline_mode=pl.Buffered(1)),
            tab_spec, tab_spec,
            full(b_gate.shape), full(conv_w.shape), full(conv_b.shape),
        ],
        out_specs=[row_spec, row_spec,
                   pl.BlockSpec((None, d, tm), lambda b, i: (b, 0, i)),
                   row_spec, row_spec,
                   pl.BlockSpec((None, tm, 2 * d), lambda b, i: (b, i, 0))],
        out_shape=[act, act, jax.ShapeDtypeStruct((bsz, d, seq), jnp.bfloat16), act, act,
                   jax.ShapeDtypeStruct((bsz, seq, 2 * d), jnp.bfloat16)],
        scratch_shapes=[pltpu.VMEM((tm + V7X_SUBLANES, d), jnp.float32)],
        compiler_params=pltpu.CompilerParams(
            dimension_semantics=("arbitrary", "arbitrary"),
            vmem_limit_bytes=V7X_VMEM_LIMIT_BYTES),
        name="proj_rotary_conv_gates",
    )(x, w_bf, cos, sin_signed, b_gate, conv_w, conv_b)


def _attn_kernel(lq1_ref, lk1_ref, lq2_ref, lk2_ref, gain_ref,
                 q_ref, k_ref, vt_ref, sa_ref, o_ref, *, lam_init):
    seq = q_ref.shape[0]
    tq = ATTN_ROWS
    bk = ATTN_KEYS
    f32 = jnp.float32
    lam = (jnp.exp(jnp.sum(lq1_ref[...] * lk1_ref[...], axis=-1, keepdims=True))
           - jnp.exp(jnp.sum(lq2_ref[...] * lk2_ref[...], axis=-1, keepdims=True))
           + lam_init)

    lane = lax.broadcasted_iota(jnp.int32, (tq, V_DIM), 1)
    map1 = lane < HEAD_DIM
    key = lax.broadcasted_iota(jnp.int32, (tq, 2 * tq), 0)
    col = lax.broadcasted_iota(jnp.int32, (tq, 2 * tq), 1)
    qpos = jnp.where(col >= tq, col - tq, col)
    allowed = (key // CHUNK) <= (qpos // CHUNK)
    gain = gain_ref[...] * (1.0 - lam_init)

    def key_blocks(i):
        blocks = [(lo, min(bk, i * tq - lo), False) for lo in range(0, i * tq, bk)]
        return blocks + [(i * tq, tq, True)]

    class Tile:
        def __init__(self, i):
            self.i = i
            self.blocks = key_blocks(i)
            self.s, self.e = [], []
            self.m = self.l = self.acc = self.ratio = None

    def stacked_q(i):
        q = q_ref[i * tq:(i + 1) * tq, :]
        zero = jnp.zeros_like(q)
        return jnp.concatenate([jnp.where(map1, q, zero), jnp.where(map1, zero, q)], axis=0)

    def score_block(t, q2, blk):
        lo, size, masked = blk
        s = lax.dot_general(k_ref[lo:lo + size, :], q2, _NT_DIMS,
                            preferred_element_type=f32)
        if masked:
            s = jnp.where(allowed, s, -jnp.inf)
        t.s.append(s)
        bm = jnp.max(s, axis=0, keepdims=True)
        t.m = bm if t.m is None else jnp.maximum(t.m, bm)

    def exp_block(t, j):
        e = jnp.exp2(t.s[j] - t.m)
        bl = jnp.sum(e, axis=0, keepdims=True)
        t.l = bl if t.l is None else t.l + bl
        t.e.append(e.astype(jnp.bfloat16))

    def pv_block(t, j):
        lo, size, _ = t.blocks[j]
        if t.ratio is None:
            t.ratio = (lam * t.l[:, :tq] / t.l[:, tq:]).astype(jnp.bfloat16)
        e = t.e[j]
        a = e[:, :tq] - t.ratio * e[:, tq:]
        pv = jnp.dot(vt_ref[:, lo:lo + size], a, preferred_element_type=f32)
        t.acc = pv if t.acc is None else t.acc + pv

    def finish(t):
        rows = slice(t.i * tq, (t.i + 1) * tq)
        o = t.acc / t.l[:, :tq]
        ms = jnp.mean(o * o, axis=0, keepdims=True)
        y = (o * lax.rsqrt(ms + RMS_EPS) * gain).T
        o_ref[rows, :] = (y * sa_ref[rows, :].astype(f32)).astype(o_ref.dtype)

    n_tiles = seq // tq
    order = list(range(1, n_tiles, 2)) + list(reversed(range(0, n_tiles, 2)))
    tiles = [Tile(i) for i in order]
    for step in range(n_tiles + 2):
        work = []
        if step < n_tiles:
            t = tiles[step]
            q2 = stacked_q(t.i)
            work.append([functools.partial(score_block, t, q2, blk) for blk in t.blocks])
        if 0 <= step - 1 < n_tiles:
            t = tiles[step - 1]
            work.append([functools.partial(exp_block, t, j) for j in range(len(t.blocks))])
        if 0 <= step - 2 < n_tiles:
            t = tiles[step - 2]
            work.append([functools.partial(pv_block, t, j) for j in range(len(t.blocks))]
                        + [functools.partial(finish, t)])
        for group in itertools.zip_longest(*work):
            for fn in group:
                if fn is not None:
                    fn()


def _attn_call(lq1, lk1, lq2, lk2, gain, q, k, vt, sa, lam_init):
    bsz, seq, width = q.shape
    n_heads = width // V_DIM
    head_spec = pl.BlockSpec((None, seq, V_DIM), lambda b, h: (b, 0, h))
    vt_spec = pl.BlockSpec((None, V_DIM, seq), lambda b, h: (b, h, 0))

    def full(shape):
        return pl.BlockSpec(shape, lambda b, h: (0,) * len(shape))

    return pl.pallas_call(
        functools.partial(_attn_kernel, lam_init=lam_init),
        grid=(bsz, n_heads),
        in_specs=[full(lq1.shape), full(lk1.shape), full(lq2.shape), full(lk2.shape),
                  full(gain.shape), head_spec, head_spec, vt_spec, head_spec],
        out_specs=head_spec,
        out_shape=jax.ShapeDtypeStruct((bsz, seq, width), jnp.bfloat16),
        compiler_params=pltpu.CompilerParams(
            dimension_semantics=("arbitrary", "arbitrary"),
            vmem_limit_bytes=V7X_VMEM_LIMIT_BYTES),
        name="diff_attention",
    )(lq1, lk1, lq2, lk2, gain, q, k, vt, sa)


def _out_kernel(oa_ref, c_ref, g_ref, x_ref, wa_ref, wb_ref, wo_ref,
                lng_ref, lnb_ref, out_ref, *, alpha):
    tm, d = x_ref.shape
    cw = OUT_COLS
    f32 = jnp.float32

    def merge_stage(rows, lo, merged):
        def dots():
            return [jnp.dot(oa_ref[rows, :], wa_ref[:, lo:lo + cw], preferred_element_type=f32),
                    jnp.dot(c_ref[rows, :], wb_ref[:, lo:lo + cw], preferred_element_type=f32)]

        def epilogue(ya, yb):
            merged.append((g_ref[rows, lo:lo + cw].astype(f32) * ya
                           + g_ref[rows, d + lo:d + lo + cw].astype(f32) * yb
                           ).astype(jnp.bfloat16))
        return dots, epilogue

    def out_stage(rows, lo, merged, resid):
        def dots():
            lhs = jnp.concatenate(merged, axis=1)
            return [jnp.dot(lhs, wo_ref[:, lo:lo + cw], preferred_element_type=f32)]

        def epilogue(out):
            resid.append(alpha * x_ref[rows, lo:lo + cw] + out)
            if lo + cw == d:
                mu = sum(jnp.sum(r, axis=-1, keepdims=True) for r in resid) / d
                dev = [r - mu for r in resid]
                var = sum(jnp.sum(v * v, axis=-1, keepdims=True) for v in dev) / d
                inv = lax.rsqrt(var + LN_EPS)
                for j, v in enumerate(dev):
                    cols = slice(j * cw, (j + 1) * cw)
                    out_ref[rows, cols] = v * inv * lng_ref[:, cols] + lnb_ref[:, cols]
        return dots, epilogue

    col_starts = range(0, d, cw)
    sub_tiles = []
    for r0 in range(0, tm, OUT_SUB_ROWS):
        rows = slice(r0, r0 + OUT_SUB_ROWS)
        merged, resid = [], []
        sub_tiles.append(([merge_stage(rows, lo, merged) for lo in col_starts],
                          [out_stage(rows, lo, merged, resid) for lo in col_starts]))
    assert len(sub_tiles) >= 2
    stages = list(sub_tiles[0][0])
    for prev, cur in zip(sub_tiles[:-1], sub_tiles[1:]):
        for m_stage, o_stage in zip(cur[0], prev[1]):
            stages += [m_stage, o_stage]
    stages += sub_tiles[-1][1]
    _run_pipelined(stages)


def _out_call(oa, c, g, x, wa, wb, wo, ln_g, ln_b, alpha):
    bsz, seq, d = x.shape
    tm = OUT_ROWS
    row_spec = pl.BlockSpec((None, tm, d), lambda b, i: (b, i, 0))

    def full(shape, **kwargs):
        return pl.BlockSpec(shape, lambda b, i: (0,) * len(shape), **kwargs)

    resident = dict(pipeline_mode=pl.Buffered(1))
    return pl.pallas_call(
        functools.partial(_out_kernel, alpha=alpha),
        grid=(bsz, seq // tm),
        in_specs=[row_spec, row_spec,
                  pl.BlockSpec((None, tm, 2 * d), lambda b, i: (b, i, 0)),
                  row_spec, full(wa.shape, **resident), full(wb.shape, **resident),
                  full(wo.shape, **resident), full(ln_g.shape), full(ln_b.shape)],
        out_specs=row_spec,
        out_shape=jax.ShapeDtypeStruct((bsz, seq, d), jnp.float32),
        compiler_params=pltpu.CompilerParams(
            dimension_semantics=("arbitrary", "arbitrary"),
            vmem_limit_bytes=V7X_VMEM_LIMIT_BYTES),
        name="merge_out_layernorm",
    )(oa, c, g, x, wa, wb, wo, ln_g, ln_b)


def _rotary_tables(seq_len):
    half = HEAD_DIM // 2
    inv_freq = 1.0 / (ROPE_THETA ** (jnp.arange(half, dtype=jnp.float32) / half))
    pos = jnp.arange(seq_len, dtype=jnp.float32)
    ang = pos[:, None] * inv_freq[None, :]
    cos = jnp.cos(ang)
    sin = jnp.sin(ang)
    reps = V7X_LANES // HEAD_DIM
    cos_t = jnp.tile(jnp.concatenate([cos, cos], -1), (1, reps))
    sin_t = jnp.tile(jnp.concatenate([-sin, sin], -1), (1, reps))
    return cos_t, sin_t


def kernel(x, w_in, b_gate, lambda_q1, lambda_k1, lambda_q2, lambda_k2, subln_g,
           conv_w, conv_b, w_a_out, w_b_out, w_o, ln_g, ln_b):
    depth = w_in.shape[0]
    seq_len = x.shape[1]
    alpha = (2.0 * depth) ** 0.25
    cos_t, sin_t = _rotary_tables(seq_len)
    bf16 = jnp.bfloat16
    for l in range(depth):
        lam_init = 0.8 - 0.6 * math.exp(-0.3 * l)
        q, k, vt, sa, c, g = _proj_call(
            x, w_in[l].astype(bf16), cos_t, sin_t,
            b_gate[l][None, :], conv_w[l], conv_b[l][None, :])
        oa = _attn_call(lambda_q1[l][None, :], lambda_k1[l][None, :],
                        lambda_q2[l][None, :], lambda_k2[l][None, :],
                        subln_g[l][:, None], q, k, vt, sa, lam_init)
        x = _out_call(oa, c, g, x, w_a_out[l].astype(bf16), w_b_out[l].astype(bf16),
                      w_o[l].astype(bf16), ln_g[l][None, :], ln_b[l][None, :], alpha)
    return x
```

```python
import functools
import itertools
import math

import jax
import jax.numpy as jnp
from jax import lax
from jax.experimental import pallas as pl
from jax.experimental.pallas import tpu as pltpu

CHUNK = 64
HEAD_DIM = 64
V_DIM = 2 * HEAD_DIM
ROPE_THETA = 10000.0
LN_EPS = 1e-5
RMS_EPS = 1e-5
CONV_K = 3

V7X_LANES = 128
V7X_SUBLANES = 8
V7X_VMEM_LIMIT_BYTES = 56 * 1024 * 1024

PROJ_ROWS = 512
PROJ_COLS = 256
ATTN_ROWS = 256
ATTN_KEYS = 512
SOFTMAX_HEADROOM = 64.0
SOFTMAX_L_MIN = 2.0 ** -80
SOFTMAX_L_MAX = 2.0 ** 100
OUT_ROWS = 1024
OUT_SUB_ROWS = 256
OUT_COLS = 256

_NT_DIMS = (((1,), (1,)), ((), ()))


def _sigmoid(z):
    return 1.0 / (1.0 + jnp.exp(-z))


def _run_pipelined(stages):
    pending = None
    for dots, epilogue in stages:
        acc = dots()
        if pending is not None:
            pending[0](*pending[1])
        pending = (epilogue, acc)
    pending[0](*pending[1])


def _proj_kernel(x_ref, w_ref, cos_ref, sin_ref, bg_ref, cw_ref, cb_ref,
                 q_ref, k_ref, vt_ref, sa_ref, c_ref, g_ref, ubuf, *, d_model):
    tm = x_ref.shape[0]
    d = d_model
    cw = PROJ_COLS
    xb = x_ref[...].astype(jnp.bfloat16)

    def proj(lo):
        return jnp.dot(xb, w_ref[:, lo:lo + cw], preferred_element_type=jnp.float32)

    cos = cos_ref[...]
    sin_signed = sin_ref[...]
    lane = lax.broadcasted_iota(jnp.int32, (tm, V7X_LANES), 1)
    first_half = (lane & (HEAD_DIM - 1)) < (HEAD_DIM // 2)

    def rotary(t):
        partner = jnp.where(first_half,
                            pltpu.roll(t, V7X_LANES - HEAD_DIM // 2, 1),
                            pltpu.roll(t, HEAD_DIM // 2, 1))
        return t * cos + partner * sin_signed

    scale = HEAD_DIM ** -0.5 * math.log2(math.e)

    def rotary_epilogue(out_ref, lo, mult):
        def epilogue(t):
            for j in range(0, cw, V7X_LANES):
                r = rotary(t[:, j:j + V7X_LANES])
                if mult != 1.0:
                    r = r * mult
                out_ref[:, lo + j:lo + j + V7X_LANES] = r.astype(out_ref.dtype)
        return epilogue

    def v_epilogue(lo):
        def epilogue(t):
            vt_ref[lo:lo + cw, :] = t.T.astype(vt_ref.dtype)
        return epilogue

    def silu_epilogue(lo):
        def epilogue(za):
            sa_ref[:, lo:lo + cw] = (za * _sigmoid(za)).astype(sa_ref.dtype)
        return epilogue

    def conv_in_epilogue(lo):
        def epilogue(h, cgate):
            ubuf[V7X_SUBLANES:V7X_SUBLANES + tm, lo:lo + cw] = cgate * h
        return epilogue

    def conv_out_epilogue(lo):
        sl = slice(lo, lo + cw)

        def epilogue(bgate, zb):
            u = ubuf[V7X_SUBLANES:V7X_SUBLANES + tm, sl]
            u1 = ubuf[V7X_SUBLANES - 1:V7X_SUBLANES - 1 + tm, sl]
            u2 = ubuf[V7X_SUBLANES - 2:V7X_SUBLANES - 2 + tm, sl]
            conv = (cw_ref[0:1, sl] * u2 + cw_ref[1:2, sl] * u1
                    + cw_ref[2:3, sl] * u + cb_ref[:, sl])
            c_ref[:, sl] = (bgate * conv * (zb * _sigmoid(zb))).astype(c_ref.dtype)
            ubuf[0:V7X_SUBLANES, sl] = ubuf[tm:tm + V7X_SUBLANES, sl]
        return epilogue

    def gate_epilogue(lo):
        def epilogue(gl):
            g_ref[:, lo:lo + cw] = _sigmoid(gl + bg_ref[:, lo:lo + cw]).astype(g_ref.dtype)
        return epilogue

    stages = []
    for lo in range(0, d, cw):
        stages.append(((lo,), rotary_epilogue(q_ref, lo, scale)))
        stages.append(((d + lo,), rotary_epilogue(k_ref, lo, 1.0)))
    for lo in range(0, d, cw):
        stages.append(((3 * d + lo,), silu_epilogue(lo)))
    for lo in range(0, d, cw):
        stages.append(((4 * d + lo, 6 * d + lo), conv_in_epilogue(lo)))
        stages.append(((5 * d + lo, 7 * d + lo), conv_out_epilogue(lo)))
    for lo in range(0, 2 * d, cw):
        stages.append(((8 * d + lo,), gate_epilogue(lo)))
    for lo in range(0, d, cw):
        stages.append(((2 * d + lo,), v_epilogue(lo)))

    @pl.when(pl.program_id(1) == 0)
    def _():
        ubuf[0:V7X_SUBLANES, :] = jnp.zeros((V7X_SUBLANES, d), jnp.float32)

    _run_pipelined([(functools.partial(lambda cols: [proj(lo) for lo in cols], cols), epilogue)
                    for cols, epilogue in stages])


def _proj_call(x, w_bf, cos, sin_signed, b_gate, conv_w, conv_b):
    bsz, seq, d = x.shape
    tm = PROJ_ROWS
    act = jax.ShapeDtypeStruct((bsz, seq, d), jnp.bfloat16)
    row_spec = pl.BlockSpec((None, tm, d), lambda b, i: (b, i, 0))
    tab_spec = pl.BlockSpec((tm, V7X_LANES), lambda b, i: (i, 0))

    def full(shape):
        return pl.BlockSpec(shape, lambda b, i: (0,) * len(shape))

    return pl.pallas_call(
        functools.partial(_proj_kernel, d_model=d),
        grid=(bsz, seq // tm),
        in_specs=[
            row_spec,
            pl.BlockSpec(w_bf.shape, lambda b, i: (0, 0), pipeline_mode=pl.Buffered(1)),
            tab_spec, tab_spec,
            full(b_gate.shape), full(conv_w.shape), full(conv_b.shape),
        ],
        out_specs=[row_spec, row_spec,
                   pl.BlockSpec((None, d, tm), lambda b, i: (b, 0, i)),
                   row_spec, row_spec,
                   pl.BlockSpec((None, tm, 2 * d), lambda b, i: (b, i, 0))],
        out_shape=[act, act, jax.ShapeDtypeStruct((bsz, d, seq), jnp.bfloat16), act, act,
                   jax.ShapeDtypeStruct((bsz, seq, 2 * d), jnp.bfloat16)],
        scratch_shapes=[pltpu.VMEM((tm + V7X_SUBLANES, d), jnp.float32)],
        compiler_params=pltpu.CompilerParams(
            dimension_semantics=("arbitrary", "arbitrary"),
            vmem_limit_bytes=V7X_VMEM_LIMIT_BYTES),
        name="proj_rotary_conv_gates",
    )(x, w_bf, cos, sin_signed, b_gate, conv_w, conv_b)


def _attn_kernel(lq1_ref, lk1_ref, lq2_ref, lk2_ref, gain_ref,
                 q_ref, k_ref, vt_ref, sa_ref, o_ref, *, lam_init):
    seq = q_ref.shape[0]
    tq = ATTN_ROWS
    bk = ATTN_KEYS
    f32 = jnp.float32
    lam = (jnp.exp(jnp.sum(lq1_ref[...] * lk1_ref[...], axis=-1, keepdims=True))
           - jnp.exp(jnp.sum(lq2_ref[...] * lk2_ref[...], axis=-1, keepdims=True))
           + lam_init)

    lane = lax.broadcasted_iota(jnp.int32, (tq, V_DIM), 1)
    map1 = lane < HEAD_DIM
    key = lax.broadcasted_iota(jnp.int32, (tq, 2 * tq), 0)
    col = lax.broadcasted_iota(jnp.int32, (tq, 2 * tq), 1)
    qpos = jnp.where(col >= tq, col - tq, col)
    allowed = (key // CHUNK) <= (qpos // CHUNK)
    gain = gain_ref[...] * (1.0 - lam_init)

    def key_blocks(i):
        blocks = [(lo, min(bk, i * tq - lo), False) for lo in range(0, i * tq, bk)]
        return blocks + [(i * tq, tq, True)]

    class Tile:
        def __init__(self, i):
            self.i = i
            self.blocks = key_blocks(i)
            self.s, self.e = [], []
            self.m = self.l = self.acc = self.ratio = None

    def stacked_q(i):
        q = q_ref[i * tq:(i + 1) * tq, :]
        zero = jnp.zeros_like(q)
        return jnp.concatenate([jnp.where(map1, q, zero), jnp.where(map1, zero, q)], axis=0)

    def score_block(t, q2, blk):
        lo, size, masked = blk
        s = lax.dot_general(k_ref[lo:lo + size, :], q2, _NT_DIMS,
                            preferred_element_type=f32)
        if masked:
            s = jnp.where(allowed, s, -jnp.inf)
        t.s.append(s)
        bm = jnp.max(s, axis=0, keepdims=True)
        t.m = bm if t.m is None else jnp.maximum(t.m, bm)

    def exp_block(t, j):
        e = jnp.exp2(t.s[j] - t.m)
        bl = jnp.sum(e, axis=0, keepdims=True)
        t.l = bl if t.l is None else t.l + bl
        t.e.append(e.astype(jnp.bfloat16))

    def pv_block(t, j):
        lo, size, _ = t.blocks[j]
        if t.ratio is None:
            t.ratio = (lam * t.l[:, :tq] / t.l[:, tq:]).astype(jnp.bfloat16)
        e = t.e[j]
        a = e[:, :tq] - t.ratio * e[:, tq:]
        pv = jnp.dot(vt_ref[:, lo:lo + size], a, preferred_element_type=f32)
        t.acc = pv if t.acc is None else t.acc + pv

    def finish(t):
        rows = slice(t.i * tq, (t.i + 1) * tq)
        o = t.acc / t.l[:, :tq]
        ms = jnp.mean(o * o, axis=0, keepdims=True)
        y = (o * lax.rsqrt(ms + RMS_EPS) * gain).T
        o_ref[rows, :] = (y * sa_ref[rows, :].astype(f32)).astype(o_ref.dtype)

    def interleave(*work):
        for group in itertools.zip_longest(*work):
            for fn in group:
                if fn is not None:
                    fn()

    n_tiles = seq // tq

    kabs_max = jnp.max(jnp.abs(k_ref[...].astype(f32)), axis=0, keepdims=True)
    lane_q = lax.broadcasted_iota(jnp.int32, (2 * tq, V7X_LANES), 1)
    ones_lane0 = {
        size: jnp.where(lax.broadcasted_iota(jnp.int32, (size, V7X_LANES), 1) == 0,
                        1.0, 0.0).astype(jnp.bfloat16)
        for size in {tq, bk}}

    def shifted_q(i):
        q2 = stacked_q(i)
        bound = jnp.sum(jnp.abs(q2.astype(f32)) * kabs_max, axis=-1, keepdims=True)
        neg_shift = jnp.where(lane_q == 0, SOFTMAX_HEADROOM - bound, 0.0)
        return jnp.concatenate([q2, neg_shift.astype(jnp.bfloat16)], axis=1)

    def shifted_exp_block(t, q2a, blk):
        lo, size, masked = blk
        k_aug = jnp.concatenate([k_ref[lo:lo + size, :], ones_lane0[size]], axis=1)
        s = lax.dot_general(k_aug, q2a, _NT_DIMS, preferred_element_type=f32)
        if masked:
            s = jnp.where(allowed, s, -jnp.inf)
        e = jnp.exp2(s)
        bl = jnp.sum(e, axis=0, keepdims=True)
        t.l = bl if t.l is None else t.l + bl
        t.e.append(e.astype(jnp.bfloat16))

    order = list(range(1, n_tiles, 2)) + list(reversed(range(0, n_tiles, 2)))
    tiles = [Tile(i) for i in order]
    for step in range(n_tiles + 1):
        work = []
        if step < n_tiles:
            t = tiles[step]
            q2a = shifted_q(t.i)
            work.append([functools.partial(shifted_exp_block, t, q2a, blk) for blk in t.blocks])
        if 0 <= step - 1 < n_tiles:
            t = tiles[step - 1]
            work.append([functools.partial(pv_block, t, j) for j in range(len(t.blocks))]
                        + [functools.partial(finish, t)])
        interleave(*work)

    l_min = functools.reduce(jnp.minimum, [t.l for t in tiles])
    l_max = functools.reduce(jnp.maximum, [t.l for t in tiles])
    in_range = jnp.logical_and(jnp.min(l_min) >= SOFTMAX_L_MIN, jnp.max(l_max) <= SOFTMAX_L_MAX)

    @pl.when(jnp.logical_not(in_range))
    def _():
        for i in range(n_tiles):
            t = Tile(i)
            q2 = stacked_q(i)
            for blk in t.blocks:
                score_block(t, q2, blk)
            for j in range(len(t.blocks)):
                exp_block(t, j)
            for j in range(len(t.blocks)):
                pv_block(t, j)
            finish(t)


def _attn_call(lq1, lk1, lq2, lk2, gain, q, k, vt, sa, lam_init):
    bsz, seq, width = q.shape
    n_heads = width // V_DIM
    head_spec = pl.BlockSpec((None, seq, V_DIM), lambda b, h: (b, 0, h))
    vt_spec = pl.BlockSpec((None, V_DIM, seq), lambda b, h: (b, h, 0))

    def full(shape):
        return pl.BlockSpec(shape, lambda b, h: (0,) * len(shape))

    return pl.pallas_call(
        functools.partial(_attn_kernel, lam_init=lam_init),
        grid=(bsz, n_heads),
        in_specs=[full(lq1.shape), full(lk1.shape), full(lq2.shape), full(lk2.shape),
                  full(gain.shape), head_spec, head_spec, vt_spec, head_spec],
        out_specs=head_spec,
        out_shape=jax.ShapeDtypeStruct((bsz, seq, width), jnp.bfloat16),
        compiler_params=pltpu.CompilerParams(
            dimension_semantics=("arbitrary", "arbitrary"),
            vmem_limit_bytes=V7X_VMEM_LIMIT_BYTES),
        name="diff_attention",
    )(lq1, lk1, lq2, lk2, gain, q, k, vt, sa)


def _out_kernel(oa_ref, c_ref, g_ref, x_ref, wa_ref, wb_ref, wo_ref,
                lng_ref, lnb_ref, out_ref, *, alpha):
    tm, d = x_ref.shape
    cw = OUT_COLS
    f32 = jnp.float32

    def merge_stage(rows, lo, merged):
        def dots():
            return [jnp.dot(oa_ref[rows, :], wa_ref[:, lo:lo + cw], preferred_element_type=f32),
                    jnp.dot(c_ref[rows, :], wb_ref[:, lo:lo + cw], preferred_element_type=f32)]

        def epilogue(ya, yb):
            merged.append((g_ref[rows, lo:lo + cw].astype(f32) * ya
                           + g_ref[rows, d + lo:d + lo + cw].astype(f32) * yb
                           ).astype(jnp.bfloat16))
        return dots, epilogue

    def out_stage(rows, lo, merged, resid):
        def dots():
            lhs = jnp.concatenate(merged, axis=1)
            return [jnp.dot(lhs, wo_ref[:, lo:lo + cw], preferred_element_type=f32)]

        def epilogue(out):
            resid.append(alpha * x_ref[rows, lo:lo + cw] + out)
            if lo + cw == d:
                mu = sum(jnp.sum(r, axis=-1, keepdims=True) for r in resid) / d
                dev = [r - mu for r in resid]
                var = sum(jnp.sum(v * v, axis=-1, keepdims=True) for v in dev) / d
                inv = lax.rsqrt(var + LN_EPS)
                for j, v in enumerate(dev):
                    cols = slice(j * cw, (j + 1) * cw)
                    out_ref[rows, cols] = v * inv * lng_ref[:, cols] + lnb_ref[:, cols]
        return dots, epilogue

    col_starts = range(0, d, cw)
    sub_tiles = []
    for r0 in range(0, tm, OUT_SUB_ROWS):
        rows = slice(r0, r0 + OUT_SUB_ROWS)
        merged, resid = [], []
        sub_tiles.append(([merge_stage(rows, lo, merged) for lo in col_starts],
                          [out_stage(rows, lo, merged, resid) for lo in col_starts]))
    assert len(sub_tiles) >= 2
    stages = list(sub_tiles[0][0])
    for prev, cur in zip(sub_tiles[:-1], sub_tiles[1:]):
        for m_stage, o_stage in zip(cur[0], prev[1]):
            stages += [m_stage, o_stage]
    stages += sub_tiles[-1][1]
    _run_pipelined(stages)


def _out_call(oa, c, g, x, wa, wb, wo, ln_g, ln_b, alpha):
    bsz, seq, d = x.shape
    tm = OUT_ROWS
    row_spec = pl.BlockSpec((None, tm, d), lambda b, i: (b, i, 0))

    def full(shape, **kwargs):
        return pl.BlockSpec(shape, lambda b, i: (0,) * len(shape), **kwargs)

    resident = dict(pipeline_mode=pl.Buffered(1))
    return pl.pallas_call(
        functools.partial(_out_kernel, alpha=alpha),
        grid=(bsz, seq // tm),
        in_specs=[row_spec, row_spec,
                  pl.BlockSpec((None, tm, 2 * d), lambda b, i: (b, i, 0)),
                  row_spec, full(wa.shape, **resident), full(wb.shape, **resident),
                  full(wo.shape, **resident), full(ln_g.shape), full(ln_b.shape)],
        out_specs=row_spec,
        out_shape=jax.ShapeDtypeStruct((bsz, seq, d), jnp.float32),
        compiler_params=pltpu.CompilerParams(
            dimension_semantics=("arbitrary", "arbitrary"),
            vmem_limit_bytes=V7X_VMEM_LIMIT_BYTES),
        name="merge_out_layernorm",
    )(oa, c, g, x, wa, wb, wo, ln_g, ln_b)


def _rotary_tables(seq_len):
    half = HEAD_DIM // 2
    inv_freq = 1.0 / (ROPE_THETA ** (jnp.arange(half, dtype=jnp.float32) / half))
    pos = jnp.arange(seq_len, dtype=jnp.float32)
    ang = pos[:, None] * inv_freq[None, :]
    cos = jnp.cos(ang)
    sin = jnp.sin(ang)
    reps = V7X_LANES // HEAD_DIM
    cos_t = jnp.tile(jnp.concatenate([cos, cos], -1), (1, reps))
    sin_t = jnp.tile(jnp.concatenate([-sin, sin], -1), (1, reps))
    return cos_t, sin_t


def kernel(x, w_in, b_gate, lambda_q1, lambda_k1, lambda_q2, lambda_k2, subln_g,
           conv_w, conv_b, w_a_out, w_b_out, w_o, ln_g, ln_b):
    depth = w_in.shape[0]
    seq_len = x.shape[1]
    alpha = (2.0 * depth) ** 0.25
    cos_t, sin_t = _rotary_tables(seq_len)
    bf16 = jnp.bfloat16
    for l in range(depth):
        lam_init = 0.8 - 0.6 * math.exp(-0.3 * l)
        q, k, vt, sa, c, g = _proj_call(
            x, w_in[l].astype(bf16), cos_t, sin_t,
            b_gate[l][None, :], conv_w[l], conv_b[l][None, :])
        oa = _attn_call(lambda_q1[l][None, :], lambda_k1[l][None, :],
                        lambda_q2[l][None, :], lambda_k2[l][None, :],
                        subln_g[l][:, None], q, k, vt, sa, lam_init)
        x = _out_call(oa, c, g, x, w_a_out[l].astype(bf16), w_b_out[l].astype(bf16),
                      w_o[l].astype(bf16), ln_g[l][None, :], ln_b[l][None, :], alpha)
    return x
```

```python
import functools
import itertools
import math

import jax
import jax.numpy as jnp
from jax import lax
from jax.experimental import pallas as pl
from jax.experimental.pallas import tpu as pltpu

CHUNK = 64
HEAD_DIM = 64
V_DIM = 2 * HEAD_DIM
ROPE_THETA = 10000.0
LN_EPS = 1e-5
RMS_EPS = 1e-5
CONV_K = 3

V7X_LANES = 128
V7X_SUBLANES = 8
V7X_VMEM_LIMIT_BYTES = 56 * 1024 * 1024

PROJ_ROWS = 512
PROJ_COLS = 256
ATTN_ROWS = 256
ATTN_KEYS = 2048
SOFTMAX_HEADROOM = 64.0
SOFTMAX_L_MIN = 2.0 ** -80
SOFTMAX_L_MAX = 2.0 ** 100
OUT_ROWS = 1024
OUT_SUB_ROWS = 256
OUT_COLS = 256

_NT_DIMS = (((1,), (1,)), ((), ()))


def _sigmoid(z):
    return 1.0 / (1.0 + jnp.exp(-z))


def _run_pipelined(stages):
    pending = None
    for dots, epilogue in stages:
        acc = dots()
        if pending is not None:
            pending[0](*pending[1])
        pending = (epilogue, acc)
    pending[0](*pending[1])


def _proj_kernel(x_ref, w_ref, cos_ref, sin_ref, bg_ref, cw_ref, cb_ref,
                 q_ref, k_ref, vt_ref, sa_ref, c_ref, g_ref, kmax_ref, ubuf, *, d_model):
    tm = x_ref.shape[0]
    d = d_model
    cw = PROJ_COLS
    xb = x_ref[...].astype(jnp.bfloat16)

    def proj(lo):
        return jnp.dot(xb, w_ref[:, lo:lo + cw], preferred_element_type=jnp.float32)

    cos = cos_ref[...]
    sin_signed = sin_ref[...]
    lane = lax.broadcasted_iota(jnp.int32, (tm, V7X_LANES), 1)
    first_half = (lane & (HEAD_DIM - 1)) < (HEAD_DIM // 2)

    def rotary(t):
        partner = jnp.where(first_half,
                            pltpu.roll(t, V7X_LANES - HEAD_DIM // 2, 1),
                            pltpu.roll(t, HEAD_DIM // 2, 1))
        return t * cos + partner * sin_signed

    scale = HEAD_DIM ** -0.5 * math.log2(math.e)

    def rotary_epilogue(out_ref, lo, mult, absmax_ref=None):
        def epilogue(t):
            for j in range(0, cw, V7X_LANES):
                sl = slice(lo + j, lo + j + V7X_LANES)
                r = rotary(t[:, j:j + V7X_LANES])
                if mult != 1.0:
                    r = r * mult
                out_ref[:, sl] = r.astype(out_ref.dtype)
                if absmax_ref is not None:
                    absmax_ref[:, sl] = jnp.maximum(
                        absmax_ref[:, sl], jnp.max(jnp.abs(r), axis=0, keepdims=True))
        return epilogue

    def v_epilogue(lo):
        def epilogue(t):
            vt_ref[lo:lo + cw, :] = t.T.astype(vt_ref.dtype)
        return epilogue

    def silu_epilogue(lo):
        def epilogue(za):
            sa_ref[:, lo:lo + cw] = (za * _sigmoid(za)).astype(sa_ref.dtype)
        return epilogue

    def conv_in_epilogue(lo):
        def epilogue(h, cgate):
            ubuf[V7X_SUBLANES:V7X_SUBLANES + tm, lo:lo + cw] = cgate * h
        return epilogue

    def conv_out_epilogue(lo):
        sl = slice(lo, lo + cw)

        def epilogue(bgate, zb):
            u = ubuf[V7X_SUBLANES:V7X_SUBLANES + tm, sl]
            u1 = ubuf[V7X_SUBLANES - 1:V7X_SUBLANES - 1 + tm, sl]
            u2 = ubuf[V7X_SUBLANES - 2:V7X_SUBLANES - 2 + tm, sl]
            conv = (cw_ref[0:1, sl] * u2 + cw_ref[1:2, sl] * u1
                    + cw_ref[2:3, sl] * u + cb_ref[:, sl])
            c_ref[:, sl] = (bgate * conv * (zb * _sigmoid(zb))).astype(c_ref.dtype)
            ubuf[0:V7X_SUBLANES, sl] = ubuf[tm:tm + V7X_SUBLANES, sl]
        return epilogue

    def gate_epilogue(lo):
        def epilogue(gl):
            g_ref[:, lo:lo + cw] = _sigmoid(gl + bg_ref[:, lo:lo + cw]).astype(g_ref.dtype)
        return epilogue

    chunks = range(0, d, cw)
    heavy = ([((3 * d + lo,), silu_epilogue(lo)) for lo in chunks]
             + [((8 * d + lo,), gate_epilogue(lo)) for lo in range(0, 2 * d, cw)]
             + [((5 * d + lo, 7 * d + lo), conv_out_epilogue(lo)) for lo in chunks])
    light = [((4 * d + lo, 6 * d + lo), conv_in_epilogue(lo)) for lo in chunks]
    for lo in chunks:
        light.append(((lo,), rotary_epilogue(q_ref, lo, scale)))
        light.append(((d + lo,), rotary_epilogue(k_ref, lo, 1.0, kmax_ref)))
    light += [((2 * d + lo,), v_epilogue(lo)) for lo in chunks]
    assert len(heavy) == len(light)
    stages = [stage for pair in zip(heavy, light) for stage in pair]

    @pl.when(pl.program_id(1) == 0)
    def _():
        ubuf[0:V7X_SUBLANES, :] = jnp.zeros((V7X_SUBLANES, d), jnp.float32)
        kmax_ref[...] = jnp.zeros_like(kmax_ref)

    _run_pipelined([(functools.partial(lambda cols: [proj(lo) for lo in cols], cols), epilogue)
                    for cols, epilogue in stages])


def _proj_call(x, w_bf, cos, sin_signed, b_gate, conv_w, conv_b):
    bsz, seq, d = x.shape
    tm = PROJ_ROWS
    act = jax.ShapeDtypeStruct((bsz, seq, d), jnp.bfloat16)
    row_spec = pl.BlockSpec((None, tm, d), lambda b, i: (b, i, 0))
    tab_spec = pl.BlockSpec((tm, V7X_LANES), lambda b, i: (i, 0))

    def full(shape):
        return pl.BlockSpec(shape, lambda b, i: (0,) * len(shape))

    return pl.pallas_call(
        functools.partial(_proj_kernel, d_model=d),
        grid=(bsz, seq // tm),
        in_specs=[
            row_spec,
            pl.BlockSpec(w_bf.shape, lambda b, i: (0, 0), pipeline_mode=pl.Buffered(1)),
            tab_spec, tab_spec,
            full(b_gate.shape), full(conv_w.shape), full(conv_b.shape),
        ],
        out_specs=[row_spec, row_spec,
                   pl.BlockSpec((None, d, tm), lambda b, i: (b, 0, i)),
                   row_spec, row_spec,
                   pl.BlockSpec((None, tm, 2 * d), lambda b, i: (b, i, 0)),
                   pl.BlockSpec((None, 1, d), lambda b, i: (b, 0, 0))],
        out_shape=[act, act, jax.ShapeDtypeStruct((bsz, d, seq), jnp.bfloat16), act, act,
                   jax.ShapeDtypeStruct((bsz, seq, 2 * d), jnp.bfloat16),
                   jax.ShapeDtypeStruct((bsz, 1, d), jnp.float32)],
        scratch_shapes=[pltpu.VMEM((tm + V7X_SUBLANES, d), jnp.float32)],
        compiler_params=pltpu.CompilerParams(
            dimension_semantics=("arbitrary", "arbitrary"),
            vmem_limit_bytes=V7X_VMEM_LIMIT_BYTES),
        name="proj_rotary_conv_gates",
    )(x, w_bf, cos, sin_signed, b_gate, conv_w, conv_b)


def _attn_kernel(lq1_ref, lk1_ref, lq2_ref, lk2_ref, gain_ref,
                 q_ref, k_ref, vt_ref, sa_ref, kmax_ref, o_ref, *, lam_init):
    seq = q_ref.shape[0]
    tq = ATTN_ROWS
    bk = ATTN_KEYS
    f32 = jnp.float32
    lam = (jnp.exp(jnp.sum(lq1_ref[...] * lk1_ref[...], axis=-1, keepdims=True))
           - jnp.exp(jnp.sum(lq2_ref[...] * lk2_ref[...], axis=-1, keepdims=True))
           + lam_init)

    lane = lax.broadcasted_iota(jnp.int32, (tq, V_DIM), 1)
    map1 = lane < HEAD_DIM
    key = lax.broadcasted_iota(jnp.int32, (tq, 2 * tq), 0)
    col = lax.broadcasted_iota(jnp.int32, (tq, 2 * tq), 1)
    qpos = jnp.where(col >= tq, col - tq, col)
    allowed = (key // CHUNK) <= (qpos // CHUNK)
    gain = gain_ref[...] * (1.0 - lam_init)

    def key_blocks(i):
        blocks = [(lo, min(bk, i * tq - lo), False) for lo in range(0, i * tq, bk)]
        return blocks + [(i * tq, tq, True)]

    class Tile:
        def __init__(self, i):
            self.i = i
            self.blocks = key_blocks(i)
            self.s, self.e = [], []
            self.m = self.l = self.acc = self.ratio = None

    def stacked_q(i):
        q = q_ref[i * tq:(i + 1) * tq, :]
        zero = jnp.zeros_like(q)
        return jnp.concatenate([jnp.where(map1, q, zero), jnp.where(map1, zero, q)], axis=0)

    def score_block(t, q2, blk):
        lo, size, masked = blk
        s = lax.dot_general(k_ref[lo:lo + size, :], q2, _NT_DIMS,
                            preferred_element_type=f32)
        if masked:
            s = jnp.where(allowed, s, -jnp.inf)
        t.s.append(s)
        bm = jnp.max(s, axis=0, keepdims=True)
        t.m = bm if t.m is None else jnp.maximum(t.m, bm)

    def exp_block(t, j):
        e = jnp.exp2(t.s[j] - t.m)
        bl = jnp.sum(e, axis=0, keepdims=True)
        t.l = bl if t.l is None else t.l + bl
        t.e.append(e.astype(jnp.bfloat16))

    def pv_block(t, j):
        lo, size, _ = t.blocks[j]
        if t.ratio is None:
            t.ratio = (lam * t.l[:, :tq] / t.l[:, tq:]).astype(jnp.bfloat16)
        e = t.e[j]
        a = e[:, :tq] - t.ratio * e[:, tq:]
        pv = jnp.dot(vt_ref[:, lo:lo + size], a, preferred_element_type=f32)
        t.acc = pv if t.acc is None else t.acc + pv

    def finish(t):
        rows = slice(t.i * tq, (t.i + 1) * tq)
        o = t.acc / t.l[:, :tq]
        ms = jnp.mean(o * o, axis=0, keepdims=True)
        y = (o * lax.rsqrt(ms + RMS_EPS) * gain).T
        o_ref[rows, :] = (y * sa_ref[rows, :].astype(f32)).astype(o_ref.dtype)

    def interleave(*work):
        for group in itertools.zip_longest(*work):
            for fn in group:
                if fn is not None:
                    fn()

    n_tiles = seq // tq

    kabs_max = kmax_ref[...].astype(jnp.bfloat16).astype(f32)
    lane_q = lax.broadcasted_iota(jnp.int32, (2 * tq, V7X_LANES), 1)
    @functools.cache
    def ones_lane0(size):
        lane_k = lax.broadcasted_iota(jnp.int32, (size, V7X_LANES), 1)
        return jnp.where(lane_k == 0, 1.0, 0.0).astype(jnp.bfloat16)

    def shifted_q(i):
        q2 = stacked_q(i)
        bound = jnp.sum(jnp.abs(q2.astype(f32)) * kabs_max, axis=-1, keepdims=True)
        neg_shift = jnp.where(lane_q == 0, SOFTMAX_HEADROOM - bound, 0.0)
        return jnp.concatenate([q2, neg_shift.astype(jnp.bfloat16)], axis=1)

    def shifted_exp_block(t, q2a, blk):
        lo, size, masked = blk
        k_aug = jnp.concatenate([k_ref[lo:lo + size, :], ones_lane0(size)], axis=1)
        s = lax.dot_general(k_aug, q2a, _NT_DIMS, preferred_element_type=f32)
        if masked:
            s = jnp.where(allowed, s, -jnp.inf)
        e = jnp.exp2(s)
        bl = jnp.sum(e, axis=0, keepdims=True)
        t.l = bl if t.l is None else t.l + bl
        t.e.append(e.astype(jnp.bfloat16))

    order = list(range(1, n_tiles, 2)) + list(reversed(range(0, n_tiles, 2)))
    tiles = [Tile(i) for i in order]
    for step in range(n_tiles + 1):
        work = []
        if step < n_tiles:
            t = tiles[step]
            q2a = shifted_q(t.i)
            work.append([functools.partial(shifted_exp_block, t, q2a, blk) for blk in t.blocks])
        else:
            l_min = functools.reduce(jnp.minimum, [t.l for t in tiles])
            l_max = functools.reduce(jnp.maximum, [t.l for t in tiles])
            in_range = jnp.logical_and(jnp.min(l_min) >= SOFTMAX_L_MIN,
                                       jnp.max(l_max) <= SOFTMAX_L_MAX)
        if 0 <= step - 1 < n_tiles:
            t = tiles[step - 1]
            work.append([functools.partial(pv_block, t, j) for j in range(len(t.blocks))]
                        + [functools.partial(finish, t)])
        interleave(*work)

    @pl.when(jnp.logical_not(in_range))
    def _():
        for i in range(n_tiles):
            t = Tile(i)
            q2 = stacked_q(i)
            for blk in t.blocks:
                score_block(t, q2, blk)
            for j in range(len(t.blocks)):
                exp_block(t, j)
            for j in range(len(t.blocks)):
                pv_block(t, j)
            finish(t)


def _attn_call(lq1, lk1, lq2, lk2, gain, q, k, vt, sa, kmax, lam_init):
    bsz, seq, width = q.shape
    n_heads = width // V_DIM
    head_spec = pl.BlockSpec((None, seq, V_DIM), lambda b, h: (b, 0, h))
    vt_spec = pl.BlockSpec((None, V_DIM, seq), lambda b, h: (b, h, 0))
    kmax_spec = pl.BlockSpec((None, 1, V_DIM), lambda b, h: (b, 0, h))

    def full(shape):
        return pl.BlockSpec(shape, lambda b, h: (0,) * len(shape))

    return pl.pallas_call(
        functools.partial(_attn_kernel, lam_init=lam_init),
        grid=(bsz, n_heads),
        in_specs=[full(lq1.shape), full(lk1.shape), full(lq2.shape), full(lk2.shape),
                  full(gain.shape), head_spec, head_spec, vt_spec, head_spec, kmax_spec],
        out_specs=head_spec,
        out_shape=jax.ShapeDtypeStruct((bsz, seq, width), jnp.bfloat16),
        compiler_params=pltpu.CompilerParams(
            dimension_semantics=("arbitrary", "arbitrary"),
            vmem_limit_bytes=V7X_VMEM_LIMIT_BYTES),
        name="diff_attention",
    )(lq1, lk1, lq2, lk2, gain, q, k, vt, sa, kmax)


def _out_kernel(oa_ref, c_ref, g_ref, x_ref, wa_ref, wb_ref, wo_ref,
                lng_ref, lnb_ref, out_ref, *, alpha):
    tm, d = x_ref.shape
    cw = OUT_COLS
    f32 = jnp.float32

    def merge_stage(rows, lo, merged):
        def dots():
            return [jnp.dot(oa_ref[rows, :], wa_ref[:, lo:lo + cw], preferred_element_type=f32),
                    jnp.dot(c_ref[rows, :], wb_ref[:, lo:lo + cw], preferred_element_type=f32)]

        def epilogue(ya, yb):
            merged.append((g_ref[rows, lo:lo + cw].astype(f32) * ya
                           + g_ref[rows, d + lo:d + lo + cw].astype(f32) * yb
                           ).astype(jnp.bfloat16))
        return dots, epilogue

    def out_stage(rows, lo, merged, resid):
        def dots():
            lhs = jnp.concatenate(merged, axis=1)
            return [jnp.dot(lhs, wo_ref[:, lo:lo + cw], preferred_element_type=f32)]

        def epilogue(out):
            resid.append(alpha * x_ref[rows, lo:lo + cw] + out)
            if lo + cw == d:
                mu = sum(jnp.sum(r, axis=-1, keepdims=True) for r in resid) / d
                dev = [r - mu for r in resid]
                var = sum(jnp.sum(v * v, axis=-1, keepdims=True) for v in dev) / d
                inv = lax.rsqrt(var + LN_EPS)
                for j, v in enumerate(dev):
                    cols = slice(j * cw, (j + 1) * cw)
                    out_ref[rows, cols] = v * inv * lng_ref[:, cols] + lnb_ref[:, cols]
        return dots, epilogue

    col_starts = range(0, d, cw)
    sub_tiles = []
    for r0 in range(0, tm, OUT_SUB_ROWS):
        rows = slice(r0, r0 + OUT_SUB_ROWS)
        merged, resid = [], []
        sub_tiles.append(([merge_stage(rows, lo, merged) for lo in col_starts],
                          [out_stage(rows, lo, merged, resid) for lo in col_starts]))
    assert len(sub_tiles) >= 2
    stages = list(sub_tiles[0][0])
    for prev, cur in zip(sub_tiles[:-1], sub_tiles[1:]):
        for m_stage, o_stage in zip(cur[0], prev[1]):
            stages += [m_stage, o_stage]
    stages += sub_tiles[-1][1]
    _run_pipelined(stages)


def _out_call(oa, c, g, x, wa, wb, wo, ln_g, ln_b, alpha):
    bsz, seq, d = x.shape
    tm = OUT_ROWS
    row_spec = pl.BlockSpec((None, tm, d), lambda b, i: (b, i, 0))

    def full(shape, **kwargs):
        return pl.BlockSpec(shape, lambda b, i: (0,) * len(shape), **kwargs)

    resident = dict(pipeline_mode=pl.Buffered(1))
    return pl.pallas_call(
        functools.partial(_out_kernel, alpha=alpha),
        grid=(bsz, seq // tm),
        in_specs=[row_spec, row_spec,
                  pl.BlockSpec((None, tm, 2 * d), lambda b, i: (b, i, 0)),
                  row_spec, full(wa.shape, **resident), full(wb.shape, **resident),
                  full(wo.shape, **resident), full(ln_g.shape), full(ln_b.shape)],
        out_specs=row_spec,
        out_shape=jax.ShapeDtypeStruct((bsz, seq, d), jnp.float32),
        compiler_params=pltpu.CompilerParams(
            dimension_semantics=("arbitrary", "arbitrary"),
            vmem_limit_bytes=V7X_VMEM_LIMIT_BYTES),
        name="merge_out_layernorm",
    )(oa, c, g, x, wa, wb, wo, ln_g, ln_b)


def _rotary_tables(seq_len):
    half = HEAD_DIM // 2
    inv_freq = 1.0 / (ROPE_THETA ** (jnp.arange(half, dtype=jnp.float32) / half))
    pos = jnp.arange(seq_len, dtype=jnp.float32)
    ang = pos[:, None] * inv_freq[None, :]
    cos = jnp.cos(ang)
    sin = jnp.sin(ang)
    reps = V7X_LANES // HEAD_DIM
    cos_t = jnp.tile(jnp.concatenate([cos, cos], -1), (1, reps))
    sin_t = jnp.tile(jnp.concatenate([-sin, sin], -1), (1, reps))
    return cos_t, sin_t


def kernel(x, w_in, b_gate, lambda_q1, lambda_k1, lambda_q2, lambda_k2, subln_g,
           conv_w, conv_b, w_a_out, w_b_out, w_o, ln_g, ln_b):
    depth = w_in.shape[0]
    seq_len = x.shape[1]
    alpha = (2.0 * depth) ** 0.25
    cos_t, sin_t = _rotary_tables(seq_len)
    bf16 = jnp.bfloat16
    for l in range(depth):
        lam_init = 0.8 - 0.6 * math.exp(-0.3 * l)
        q, k, vt, sa, c, g, kmax = _proj_call(
            x, w_in[l].astype(bf16), cos_t, sin_t,
            b_gate[l][None, :], conv_w[l], conv_b[l][None, :])
        oa = _attn_call(lambda_q1[l][None, :], lambda_k1[l][None, :],
                        lambda_q2[l][None, :], lambda_k2[l][None, :],
                        subln_g[l][:, None], q, k, vt, sa, kmax, lam_init)
        x = _out_call(oa, c, g, x, w_a_out[l].astype(bf16), w_b_out[l].astype(bf16),
                      w_o[l].astype(bf16), ln_g[l][None, :], ln_b[l][None, :], alpha)
    return x
```

```python
import functools
import itertools
import math

import jax
import jax.numpy as jnp
from jax import lax
from jax.experimental import pallas as pl
from jax.experimental.pallas import tpu as pltpu

CHUNK = 64
HEAD_DIM = 64
V_DIM = 2 * HEAD_DIM
ROPE_THETA = 10000.0
LN_EPS = 1e-5
RMS_EPS = 1e-5
CONV_K = 3

V7X_LANES = 128
V7X_SUBLANES = 8
V7X_VMEM_LIMIT_BYTES = 56 * 1024 * 1024

PROJ_ROWS = 512
PROJ_COLS = 256
ATTN_ROWS = 256
ATTN_KEYS = 2048
ATTN_HEADS_PER_STEP = 2
SOFTMAX_HEADROOM = 64.0
SOFTMAX_L_MIN = 2.0 ** -80
SOFTMAX_L_MAX = 2.0 ** 100
OUT_ROWS = 1024
OUT_SUB_ROWS = 256
OUT_COLS = 256

_NT_DIMS = (((1,), (1,)), ((), ()))


def _sigmoid(z):
    return 1.0 / (1.0 + jnp.exp(-z))


def _run_pipelined(stages):
    pending = None
    for dots, epilogue in stages:
        acc = dots()
        if pending is not None:
            pending[0](*pending[1])
        pending = (epilogue, acc)
    pending[0](*pending[1])


def _proj_kernel(x_ref, w_ref, cos_ref, sin_ref, bg_ref, cw_ref, cb_ref,
                 q_ref, k_ref, vt_ref, sa_ref, c_ref, g_ref, kmax_ref, ubuf, *, d_model):
    tm = x_ref.shape[0]
    d = d_model
    cw = PROJ_COLS
    xb = x_ref[...].astype(jnp.bfloat16)

    def proj(lo):
        return jnp.dot(xb, w_ref[:, lo:lo + cw], preferred_element_type=jnp.float32)

    cos = cos_ref[...]
    sin_signed = sin_ref[...]
    lane = lax.broadcasted_iota(jnp.int32, (tm, V7X_LANES), 1)
    first_half = (lane & (HEAD_DIM - 1)) < (HEAD_DIM // 2)

    def rotary(t):
        partner = jnp.where(first_half,
                            pltpu.roll(t, V7X_LANES - HEAD_DIM // 2, 1),
                            pltpu.roll(t, HEAD_DIM // 2, 1))
        return t * cos + partner * sin_signed

    scale = HEAD_DIM ** -0.5 * math.log2(math.e)

    def rotary_epilogue(out_ref, lo, mult, absmax_ref=None):
        def epilogue(t):
            for j in range(0, cw, V7X_LANES):
                sl = slice(lo + j, lo + j + V7X_LANES)
                r = rotary(t[:, j:j + V7X_LANES])
                if mult != 1.0:
                    r = r * mult
                out_ref[:, sl] = r.astype(out_ref.dtype)
                if absmax_ref is not None:
                    absmax_ref[:, sl] = jnp.maximum(
                        absmax_ref[:, sl], jnp.max(jnp.abs(r), axis=0, keepdims=True))
        return epilogue

    def v_epilogue(lo):
        def epilogue(t):
            vt_ref[lo:lo + cw, :] = t.T.astype(vt_ref.dtype)
        return epilogue

    def silu_epilogue(lo):
        def epilogue(za):
            sa_ref[:, lo:lo + cw] = (za * _sigmoid(za)).astype(sa_ref.dtype)
        return epilogue

    def conv_in_epilogue(lo):
        def epilogue(h, cgate):
            ubuf[V7X_SUBLANES:V7X_SUBLANES + tm, lo:lo + cw] = cgate * h
        return epilogue

    def conv_out_epilogue(lo):
        sl = slice(lo, lo + cw)

        def epilogue(bgate, zb):
            u = ubuf[V7X_SUBLANES:V7X_SUBLANES + tm, sl]
            u1 = ubuf[V7X_SUBLANES - 1:V7X_SUBLANES - 1 + tm, sl]
            u2 = ubuf[V7X_SUBLANES - 2:V7X_SUBLANES - 2 + tm, sl]
            conv = (cw_ref[0:1, sl] * u2 + cw_ref[1:2, sl] * u1
                    + cw_ref[2:3, sl] * u + cb_ref[:, sl])
            c_ref[:, sl] = (bgate * conv * (zb * _sigmoid(zb))).astype(c_ref.dtype)
            ubuf[0:V7X_SUBLANES, sl] = ubuf[tm:tm + V7X_SUBLANES, sl]
        return epilogue

    def gate_epilogue(lo):
        def epilogue(gl):
            g_ref[:, lo:lo + cw] = _sigmoid(gl + bg_ref[:, lo:lo + cw]).astype(g_ref.dtype)
        return epilogue

    chunks = range(0, d, cw)
    heavy = ([((3 * d + lo,), silu_epilogue(lo)) for lo in chunks]
             + [((8 * d + lo,), gate_epilogue(lo)) for lo in range(0, 2 * d, cw)]
             + [((5 * d + lo, 7 * d + lo), conv_out_epilogue(lo)) for lo in chunks])
    light = [((4 * d + lo, 6 * d + lo), conv_in_epilogue(lo)) for lo in chunks]
    for lo in chunks:
        light.append(((lo,), rotary_epilogue(q_ref, lo, scale)))
        light.append(((d + lo,), rotary_epilogue(k_ref, lo, 1.0, kmax_ref)))
    light += [((2 * d + lo,), v_epilogue(lo)) for lo in chunks]
    assert len(heavy) == len(light)
    stages = [stage for pair in zip(heavy, light) for stage in pair]

    @pl.when(pl.program_id(1) == 0)
    def _():
        ubuf[0:V7X_SUBLANES, :] = jnp.zeros((V7X_SUBLANES, d), jnp.float32)
        kmax_ref[...] = jnp.zeros_like(kmax_ref)

    _run_pipelined([(functools.partial(lambda cols: [proj(lo) for lo in cols], cols), epilogue)
                    for cols, epilogue in stages])


def _proj_call(x, w_bf, cos, sin_signed, b_gate, conv_w, conv_b):
    bsz, seq, d = x.shape
    tm = PROJ_ROWS
    act = jax.ShapeDtypeStruct((bsz, seq, d), jnp.bfloat16)
    row_spec = pl.BlockSpec((None, tm, d), lambda b, i: (b, i, 0))
    tab_spec = pl.BlockSpec((tm, V7X_LANES), lambda b, i: (i, 0))

    def full(shape):
        return pl.BlockSpec(shape, lambda b, i: (0,) * len(shape))

    return pl.pallas_call(
        functools.partial(_proj_kernel, d_model=d),
        grid=(bsz, seq // tm),
        in_specs=[
            row_spec,
            pl.BlockSpec(w_bf.shape, lambda b, i: (0, 0), pipeline_mode=pl.Buffered(1)),
            tab_spec, tab_spec,
            full(b_gate.shape), full(conv_w.shape), full(conv_b.shape),
        ],
        out_specs=[row_spec, row_spec,
                   pl.BlockSpec((None, d, tm), lambda b, i: (b, 0, i)),
                   row_spec, row_spec,
                   pl.BlockSpec((None, tm, 2 * d), lambda b, i: (b, i, 0)),
                   pl.BlockSpec((None, 1, d), lambda b, i: (b, 0, 0))],
        out_shape=[act, act, jax.ShapeDtypeStruct((bsz, d, seq), jnp.bfloat16), act, act,
                   jax.ShapeDtypeStruct((bsz, seq, 2 * d), jnp.bfloat16),
                   jax.ShapeDtypeStruct((bsz, 1, d), jnp.float32)],
        scratch_shapes=[pltpu.VMEM((tm + V7X_SUBLANES, d), jnp.float32)],
        compiler_params=pltpu.CompilerParams(
            dimension_semantics=("arbitrary", "arbitrary"),
            vmem_limit_bytes=V7X_VMEM_LIMIT_BYTES),
        name="proj_rotary_conv_gates",
    )(x, w_bf, cos, sin_signed, b_gate, conv_w, conv_b)


def _attn_kernel(lq1_ref, lk1_ref, lq2_ref, lk2_ref, gain_ref,
                 q_ref, k_ref, vt_ref, sa_ref, kmax_ref, o_ref, *, lam_init):
    seq = q_ref.shape[0]
    tq = ATTN_ROWS
    bk = ATTN_KEYS
    f32 = jnp.float32
    lam = (jnp.exp(jnp.sum(lq1_ref[...] * lk1_ref[...], axis=-1, keepdims=True))
           - jnp.exp(jnp.sum(lq2_ref[...] * lk2_ref[...], axis=-1, keepdims=True))
           + lam_init)

    lane = lax.broadcasted_iota(jnp.int32, (tq, V_DIM), 1)
    map1 = lane < HEAD_DIM
    key = lax.broadcasted_iota(jnp.int32, (tq, 2 * tq), 0)
    col = lax.broadcasted_iota(jnp.int32, (tq, 2 * tq), 1)
    qpos = jnp.where(col >= tq, col - tq, col)
    allowed = (key // CHUNK) <= (qpos // CHUNK)
    gain = gain_ref[...] * (1.0 - lam_init)

    def key_blocks(i):
        blocks = [(lo, min(bk, i * tq - lo), False) for lo in range(0, i * tq, bk)]
        return blocks + [(i * tq, tq, True)]

    def head_lanes(hh):
        return slice(hh * V_DIM, (hh + 1) * V_DIM)

    class Tile:
        def __init__(self, i, hh):
            self.i = i
            self.lanes = head_lanes(hh)
            self.blocks = key_blocks(i)
            self.s, self.e = [], []
            self.m = self.l = self.acc = self.ratio = None

    def stacked_q(t):
        q = q_ref[t.i * tq:(t.i + 1) * tq, t.lanes]
        zero = jnp.zeros_like(q)
        return jnp.concatenate([jnp.where(map1, q, zero), jnp.where(map1, zero, q)], axis=0)

    def score_block(t, q2, blk):
        lo, size, masked = blk
        s = lax.dot_general(k_ref[lo:lo + size, t.lanes], q2, _NT_DIMS,
                            preferred_element_type=f32)
        if masked:
            s = jnp.where(allowed, s, -jnp.inf)
        t.s.append(s)
        bm = jnp.max(s, axis=0, keepdims=True)
        t.m = bm if t.m is None else jnp.maximum(t.m, bm)

    def exp_block(t, j):
        e = jnp.exp2(t.s[j] - t.m)
        bl = jnp.sum(e, axis=0, keepdims=True)
        t.l = bl if t.l is None else t.l + bl
        t.e.append(e.astype(jnp.bfloat16))

    def pv_block(t, j):
        lo, size, _ = t.blocks[j]
        if t.ratio is None:
            t.ratio = (lam * t.l[:, :tq] / t.l[:, tq:]).astype(jnp.bfloat16)
        e = t.e[j]
        a = e[:, :tq] - t.ratio * e[:, tq:]
        pv = jnp.dot(vt_ref[t.lanes, lo:lo + size], a, preferred_element_type=f32)
        t.acc = pv if t.acc is None else t.acc + pv

    def finish(t):
        rows = slice(t.i * tq, (t.i + 1) * tq)
        o = t.acc / t.l[:, :tq]
        ms = jnp.mean(o * o, axis=0, keepdims=True)
        y = (o * lax.rsqrt(ms + RMS_EPS) * gain).T
        o_ref[rows, t.lanes] = (y * sa_ref[rows, t.lanes].astype(f32)).astype(o_ref.dtype)

    def interleave(*work):
        for group in itertools.zip_longest(*work):
            for fn in group:
                if fn is not None:
                    fn()

    n_tiles = seq // tq
    n_heads = q_ref.shape[1] // V_DIM

    kabs_max = kmax_ref[...].astype(jnp.bfloat16).astype(f32)
    lane_q = lax.broadcasted_iota(jnp.int32, (2 * tq, V7X_LANES), 1)
    @functools.cache
    def ones_lane0(size):
        lane_k = lax.broadcasted_iota(jnp.int32, (size, V7X_LANES), 1)
        return jnp.where(lane_k == 0, 1.0, 0.0).astype(jnp.bfloat16)

    def shifted_q(t):
        q2 = stacked_q(t)
        bound = jnp.sum(jnp.abs(q2.astype(f32)) * kabs_max[:, t.lanes], axis=-1, keepdims=True)
        neg_shift = jnp.where(lane_q == 0, SOFTMAX_HEADROOM - bound, 0.0)
        return jnp.concatenate([q2, neg_shift.astype(jnp.bfloat16)], axis=1)

    def shifted_exp_block(t, q2a, blk):
        lo, size, masked = blk
        k_aug = jnp.concatenate([k_ref[lo:lo + size, t.lanes], ones_lane0(size)], axis=1)
        s = lax.dot_general(k_aug, q2a, _NT_DIMS, preferred_element_type=f32)
        if masked:
            s = jnp.where(allowed, s, -jnp.inf)
        e = jnp.exp2(s)
        bl = jnp.sum(e, axis=0, keepdims=True)
        t.l = bl if t.l is None else t.l + bl
        t.e.append(e.astype(jnp.bfloat16))

    order = list(range(1, n_tiles, 2)) + list(reversed(range(0, n_tiles, 2)))
    tiles = [Tile(i, hh) for hh in range(n_heads) for i in order]
    for step in range(len(tiles) + 1):
        work = []
        if step < len(tiles):
            t = tiles[step]
            q2a = shifted_q(t)
            work.append([functools.partial(shifted_exp_block, t, q2a, blk) for blk in t.blocks])
        else:
            l_min = functools.reduce(jnp.minimum, [t.l for t in tiles])
            l_max = functools.reduce(jnp.maximum, [t.l for t in tiles])
            in_range = jnp.logical_and(jnp.min(l_min) >= SOFTMAX_L_MIN,
                                       jnp.max(l_max) <= SOFTMAX_L_MAX)
        if step >= 1:
            t = tiles[step - 1]
            work.append([functools.partial(pv_block, t, j) for j in range(len(t.blocks))]
                        + [functools.partial(finish, t)])
        interleave(*work)

    @pl.when(jnp.logical_not(in_range))
    def _():
        for hh, i in itertools.product(range(n_heads), range(n_tiles)):
            t = Tile(i, hh)
            q2 = stacked_q(t)
            for blk in t.blocks:
                score_block(t, q2, blk)
            for j in range(len(t.blocks)):
                exp_block(t, j)
            for j in range(len(t.blocks)):
                pv_block(t, j)
            finish(t)


def _attn_call(lq1, lk1, lq2, lk2, gain, q, k, vt, sa, kmax, lam_init):
    bsz, seq, width = q.shape
    n_heads = width // V_DIM
    hw = ATTN_HEADS_PER_STEP * V_DIM
    head_spec = pl.BlockSpec((None, seq, hw), lambda b, h: (b, 0, h))
    vt_spec = pl.BlockSpec((None, hw, seq), lambda b, h: (b, h, 0))
    kmax_spec = pl.BlockSpec((None, 1, hw), lambda b, h: (b, 0, h))

    def full(shape):
        return pl.BlockSpec(shape, lambda b, h: (0,) * len(shape))

    return pl.pallas_call(
        functools.partial(_attn_kernel, lam_init=lam_init),
        grid=(bsz, n_heads // ATTN_HEADS_PER_STEP),
        in_specs=[full(lq1.shape), full(lk1.shape), full(lq2.shape), full(lk2.shape),
                  full(gain.shape), head_spec, head_spec, vt_spec, head_spec, kmax_spec],
        out_specs=head_spec,
        out_shape=jax.ShapeDtypeStruct((bsz, seq, width), jnp.bfloat16),
        compiler_params=pltpu.CompilerParams(
            dimension_semantics=("arbitrary", "arbitrary"),
            vmem_limit_bytes=V7X_VMEM_LIMIT_BYTES),
        name="diff_attention",
    )(lq1, lk1, lq2, lk2, gain, q, k, vt, sa, kmax)


def _out_kernel(oa_ref, c_ref, g_ref, x_ref, wa_ref, wb_ref, wo_ref,
                lng_ref, lnb_ref, out_ref, *, alpha):
    tm, d = x_ref.shape
    cw = OUT_COLS
    f32 = jnp.float32

    def merge_stage(rows, lo, merged):
        def dots():
            return [jnp.dot(oa_ref[rows, :], wa_ref[:, lo:lo + cw], preferred_element_type=f32),
                    jnp.dot(c_ref[rows, :], wb_ref[:, lo:lo + cw], preferred_element_type=f32)]

        def epilogue(ya, yb):
            merged.append((g_ref[rows, lo:lo + cw].astype(f32) * ya
                           + g_ref[rows, d + lo:d + lo + cw].astype(f32) * yb
                           ).astype(jnp.bfloat16))
        return dots, epilogue

    def out_stage(rows, lo, merged, resid):
        def dots():
            lhs = jnp.concatenate(merged, axis=1)
            return [jnp.dot(lhs, wo_ref[:, lo:lo + cw], preferred_element_type=f32)]

        def epilogue(out):
            resid.append(alpha * x_ref[rows, lo:lo + cw] + out)
            if lo + cw == d:
                mu = sum(jnp.sum(r, axis=-1, keepdims=True) for r in resid) / d
                dev = [r - mu for r in resid]
                var = sum(jnp.sum(v * v, axis=-1, keepdims=True) for v in dev) / d
                inv = lax.rsqrt(var + LN_EPS)
                for j, v in enumerate(dev):
                    cols = slice(j * cw, (j + 1) * cw)
                    out_ref[rows, cols] = v * inv * lng_ref[:, cols] + lnb_ref[:, cols]
        return dots, epilogue

    col_starts = range(0, d, cw)
    sub_tiles = []
    for r0 in range(0, tm, OUT_SUB_ROWS):
        rows = slice(r0, r0 + OUT_SUB_ROWS)
        merged, resid = [], []
        sub_tiles.append(([merge_stage(rows, lo, merged) for lo in col_starts],
                          [out_stage(rows, lo, merged, resid) for lo in col_starts]))
    assert len(sub_tiles) >= 2
    stages = list(sub_tiles[0][0])
    for prev, cur in zip(sub_tiles[:-1], sub_tiles[1:]):
        for m_stage, o_stage in zip(cur[0], prev[1]):
            stages += [m_stage, o_stage]
    stages += sub_tiles[-1][1]
    _run_pipelined(stages)


def _out_call(oa, c, g, x, wa, wb, wo, ln_g, ln_b, alpha):
    bsz, seq, d = x.shape
    tm = OUT_ROWS
    row_spec = pl.BlockSpec((None, tm, d), lambda b, i: (b, i, 0))

    def full(shape, **kwargs):
        return pl.BlockSpec(shape, lambda b, i: (0,) * len(shape), **kwargs)

    resident = dict(pipeline_mode=pl.Buffered(1))
    return pl.pallas_call(
        functools.partial(_out_kernel, alpha=alpha),
        grid=(bsz, seq // tm),
        in_specs=[row_spec, row_spec,
                  pl.BlockSpec((None, tm, 2 * d), lambda b, i: (b, i, 0)),
                  row_spec, full(wa.shape, **resident), full(wb.shape, **resident),
                  full(wo.shape, **resident), full(ln_g.shape), full(ln_b.shape)],
        out_specs=row_spec,
        out_shape=jax.ShapeDtypeStruct((bsz, seq, d), jnp.float32),
        compiler_params=pltpu.CompilerParams(
            dimension_semantics=("arbitrary", "arbitrary"),
            vmem_limit_bytes=V7X_VMEM_LIMIT_BYTES),
        name="merge_out_layernorm",
    )(oa, c, g, x, wa, wb, wo, ln_g, ln_b)


def _rotary_tables(seq_len):
    half = HEAD_DIM // 2
    inv_freq = 1.0 / (ROPE_THETA ** (jnp.arange(half, dtype=jnp.float32) / half))
    pos = jnp.arange(seq_len, dtype=jnp.float32)
    ang = pos[:, None] * inv_freq[None, :]
    cos = jnp.cos(ang)
    sin = jnp.sin(ang)
    reps = V7X_LANES // HEAD_DIM
    cos_t = jnp.tile(jnp.concatenate([cos, cos], -1), (1, reps))
    sin_t = jnp.tile(jnp.concatenate([-sin, sin], -1), (1, reps))
    return cos_t, sin_t


def kernel(x, w_in, b_gate, lambda_q1, lambda_k1, lambda_q2, lambda_k2, subln_g,
           conv_w, conv_b, w_a_out, w_b_out, w_o, ln_g, ln_b):
    depth = w_in.shape[0]
    seq_len = x.shape[1]
    alpha = (2.0 * depth) ** 0.25
    cos_t, sin_t = _rotary_tables(seq_len)
    bf16 = jnp.bfloat16
    for l in range(depth):
        lam_init = 0.8 - 0.6 * math.exp(-0.3 * l)
        q, k, vt, sa, c, g, kmax = _proj_call(
            x, w_in[l].astype(bf16), cos_t, sin_t,
            b_gate[l][None, :], conv_w[l], conv_b[l][None, :])
        oa = _attn_call(lambda_q1[l][None, :], lambda_k1[l][None, :],
                        lambda_q2[l][None, :], lambda_k2[l][None, :],
                        subln_g[l][:, None], q, k, vt, sa, kmax, lam_init)
        x = _out_call(oa, c, g, x, w_a_out[l].astype(bf16), w_b_out[l].astype(bf16),
                      w_o[l].astype(bf16), ln_g[l][None, :], ln_b[l][None, :], alpha)
    return x
```

```python
import functools
import itertools
import math

import jax
import jax.numpy as jnp
from jax import lax
from jax.experimental import pallas as pl
from jax.experimental.pallas import tpu as pltpu

CHUNK = 64
HEAD_DIM = 64
V_DIM = 2 * HEAD_DIM
ROPE_THETA = 10000.0
LN_EPS = 1e-5
RMS_EPS = 1e-5
CONV_K = 3

V7X_LANES = 128
V7X_SUBLANES = 8
V7X_VMEM_LIMIT_BYTES = 56 * 1024 * 1024

PROJ_ROWS = 512
PROJ_COLS = 256
ATTN_ROWS = 256
ATTN_KEYS = 2048
ATTN_HEADS_PER_STEP = 4
SOFTMAX_HEADROOM = 64.0
SOFTMAX_L_MIN = 2.0 ** -80
SOFTMAX_L_MAX = 2.0 ** 100
OUT_ROWS = 1024
OUT_SUB_ROWS = 256
OUT_COLS = 256

_NT_DIMS = (((1,), (1,)), ((), ()))


def _sigmoid(z):
    return 1.0 / (1.0 + jnp.exp(-z))


def _run_pipelined(stages):
    pending = None
    for dots, epilogue in stages:
        acc = dots()
        if pending is not None:
            pending[0](*pending[1])
        pending = (epilogue, acc)
    pending[0](*pending[1])


def _proj_kernel(x_ref, w_ref, cos_ref, sin_ref, bg_ref, cw_ref, cb_ref,
                 q_ref, k_ref, vt_ref, sa_ref, c_ref, g_ref, kmax_ref, ubuf, *, d_model):
    tm = x_ref.shape[0]
    d = d_model
    cw = PROJ_COLS
    xb = x_ref[...].astype(jnp.bfloat16)

    def proj(lo):
        return jnp.dot(xb, w_ref[:, lo:lo + cw], preferred_element_type=jnp.float32)

    cos = cos_ref[...]
    sin_signed = sin_ref[...]
    lane = lax.broadcasted_iota(jnp.int32, (tm, V7X_LANES), 1)
    first_half = (lane & (HEAD_DIM - 1)) < (HEAD_DIM // 2)

    def rotary(t):
        partner = jnp.where(first_half,
                            pltpu.roll(t, V7X_LANES - HEAD_DIM // 2, 1),
                            pltpu.roll(t, HEAD_DIM // 2, 1))
        return t * cos + partner * sin_signed

    scale = HEAD_DIM ** -0.5 * math.log2(math.e)

    def rotary_epilogue(out_ref, lo, mult, absmax_ref=None):
        def epilogue(t):
            for j in range(0, cw, V7X_LANES):
                sl = slice(lo + j, lo + j + V7X_LANES)
                r = rotary(t[:, j:j + V7X_LANES])
                if mult != 1.0:
                    r = r * mult
                out_ref[:, sl] = r.astype(out_ref.dtype)
                if absmax_ref is not None:
                    absmax_ref[:, sl] = jnp.maximum(
                        absmax_ref[:, sl], jnp.max(jnp.abs(r), axis=0, keepdims=True))
        return epilogue

    def v_epilogue(lo):
        def epilogue(t):
            vt_ref[lo:lo + cw, :] = t.T.astype(vt_ref.dtype)
        return epilogue

    def silu_epilogue(lo):
        def epilogue(za):
            sa_ref[:, lo:lo + cw] = (za * _sigmoid(za)).astype(sa_ref.dtype)
        return epilogue

    def conv_in_epilogue(lo):
        def epilogue(h, cgate):
            ubuf[V7X_SUBLANES:V7X_SUBLANES + tm, lo:lo + cw] = cgate * h
        return epilogue

    def conv_out_epilogue(lo):
        sl = slice(lo, lo + cw)

        def epilogue(bgate, zb):
            u = ubuf[V7X_SUBLANES:V7X_SUBLANES + tm, sl]
            u1 = ubuf[V7X_SUBLANES - 1:V7X_SUBLANES - 1 + tm, sl]
            u2 = ubuf[V7X_SUBLANES - 2:V7X_SUBLANES - 2 + tm, sl]
            conv = (cw_ref[0:1, sl] * u2 + cw_ref[1:2, sl] * u1
                    + cw_ref[2:3, sl] * u + cb_ref[:, sl])
            c_ref[:, sl] = (bgate * conv * (zb * _sigmoid(zb))).astype(c_ref.dtype)
            ubuf[0:V7X_SUBLANES, sl] = ubuf[tm:tm + V7X_SUBLANES, sl]
        return epilogue

    def gate_epilogue(lo):
        def epilogue(gl):
            g_ref[:, lo:lo + cw] = _sigmoid(gl + bg_ref[:, lo:lo + cw]).astype(g_ref.dtype)
        return epilogue

    chunks = range(0, d, cw)
    heavy = ([((3 * d + lo,), silu_epilogue(lo)) for lo in chunks]
             + [((8 * d + lo,), gate_epilogue(lo)) for lo in range(0, 2 * d, cw)]
             + [((5 * d + lo, 7 * d + lo), conv_out_epilogue(lo)) for lo in chunks])
    light = [((4 * d + lo, 6 * d + lo), conv_in_epilogue(lo)) for lo in chunks]
    for lo in chunks:
        light.append(((lo,), rotary_epilogue(q_ref, lo, scale)))
        light.append(((d + lo,), rotary_epilogue(k_ref, lo, 1.0, kmax_ref)))
    light += [((2 * d + lo,), v_epilogue(lo)) for lo in chunks]
    assert len(heavy) == len(light)
    stages = [stage for pair in zip(heavy, light) for stage in pair]

    @pl.when(pl.program_id(1) == 0)
    def _():
        ubuf[0:V7X_SUBLANES, :] = jnp.zeros((V7X_SUBLANES, d), jnp.float32)
        kmax_ref[...] = jnp.zeros_like(kmax_ref)

    _run_pipelined([(functools.partial(lambda cols: [proj(lo) for lo in cols], cols), epilogue)
                    for cols, epilogue in stages])


def _proj_call(x, w_bf, cos, sin_signed, b_gate, conv_w, conv_b):
    bsz, seq, d = x.shape
    tm = PROJ_ROWS
    act = jax.ShapeDtypeStruct((bsz, seq, d), jnp.bfloat16)
    row_spec = pl.BlockSpec((None, tm, d), lambda b, i: (b, i, 0))
    tab_spec = pl.BlockSpec((tm, V7X_LANES), lambda b, i: (i, 0))

    def full(shape):
        return pl.BlockSpec(shape, lambda b, i: (0,) * len(shape))

    return pl.pallas_call(
        functools.partial(_proj_kernel, d_model=d),
        grid=(bsz, seq // tm),
        in_specs=[
            row_spec,
            pl.BlockSpec(w_bf.shape, lambda b, i: (0, 0), pipeline_mode=pl.Buffered(1)),
            tab_spec, tab_spec,
            full(b_gate.shape), full(conv_w.shape), full(conv_b.shape),
        ],
        out_specs=[row_spec, row_spec,
                   pl.BlockSpec((None, d, tm), lambda b, i: (b, 0, i)),
                   row_spec, row_spec,
                   pl.BlockSpec((None, tm, 2 * d), lambda b, i: (b, i, 0)),
                   pl.BlockSpec((None, 1, d), lambda b, i: (b, 0, 0))],
        out_shape=[act, act, jax.ShapeDtypeStruct((bsz, d, seq), jnp.bfloat16), act, act,
                   jax.ShapeDtypeStruct((bsz, seq, 2 * d), jnp.bfloat16),
                   jax.ShapeDtypeStruct((bsz, 1, d), jnp.float32)],
        scratch_shapes=[pltpu.VMEM((tm + V7X_SUBLANES, d), jnp.float32)],
        compiler_params=pltpu.CompilerParams(
            dimension_semantics=("arbitrary", "arbitrary"),
            vmem_limit_bytes=V7X_VMEM_LIMIT_BYTES),
        name="proj_rotary_conv_gates",
    )(x, w_bf, cos, sin_signed, b_gate, conv_w, conv_b)


def _attn_kernel(lq1_ref, lk1_ref, lq2_ref, lk2_ref, gain_ref,
                 q_ref, k_ref, vt_ref, sa_ref, kmax_ref, o_ref, *, lam_init):
    seq = q_ref.shape[0]
    tq = ATTN_ROWS
    bk = ATTN_KEYS
    f32 = jnp.float32
    lam = (jnp.exp(jnp.sum(lq1_ref[...] * lk1_ref[...], axis=-1, keepdims=True))
           - jnp.exp(jnp.sum(lq2_ref[...] * lk2_ref[...], axis=-1, keepdims=True))
           + lam_init)

    lane = lax.broadcasted_iota(jnp.int32, (tq, V_DIM), 1)
    map1 = lane < HEAD_DIM
    key = lax.broadcasted_iota(jnp.int32, (tq, 2 * tq), 0)
    col = lax.broadcasted_iota(jnp.int32, (tq, 2 * tq), 1)
    qpos = jnp.where(col >= tq, col - tq, col)
    allowed = (key // CHUNK) <= (qpos // CHUNK)
    gain = gain_ref[...] * (1.0 - lam_init)

    def key_blocks(i):
        blocks = [(lo, min(bk, i * tq - lo), False) for lo in range(0, i * tq, bk)]
        return blocks + [(i * tq, tq, True)]

    def head_lanes(hh):
        return slice(hh * V_DIM, (hh + 1) * V_DIM)

    class Tile:
        def __init__(self, i, hh):
            self.i = i
            self.lanes = head_lanes(hh)
            self.blocks = key_blocks(i)
            self.s, self.e = [], []
            self.m = self.l = self.acc = self.ratio = None

    def stacked_q(t):
        q = q_ref[t.i * tq:(t.i + 1) * tq, t.lanes]
        zero = jnp.zeros_like(q)
        return jnp.concatenate([jnp.where(map1, q, zero), jnp.where(map1, zero, q)], axis=0)

    def score_block(t, q2, blk):
        lo, size, masked = blk
        s = lax.dot_general(k_ref[lo:lo + size, t.lanes], q2, _NT_DIMS,
                            preferred_element_type=f32)
        if masked:
            s = jnp.where(allowed, s, -jnp.inf)
        t.s.append(s)
        bm = jnp.max(s, axis=0, keepdims=True)
        t.m = bm if t.m is None else jnp.maximum(t.m, bm)

    def exp_block(t, j):
        e = jnp.exp2(t.s[j] - t.m)
        bl = jnp.sum(e, axis=0, keepdims=True)
        t.l = bl if t.l is None else t.l + bl
        t.e.append(e.astype(jnp.bfloat16))

    def pv_block(t, j):
        lo, size, _ = t.blocks[j]
        if t.ratio is None:
            t.ratio = (lam * t.l[:, :tq] / t.l[:, tq:]).astype(jnp.bfloat16)
        e = t.e[j]
        a = e[:, :tq] - t.ratio * e[:, tq:]
        pv = jnp.dot(vt_ref[t.lanes, lo:lo + size], a, preferred_element_type=f32)
        t.acc = pv if t.acc is None else t.acc + pv

    def finish(t):
        rows = slice(t.i * tq, (t.i + 1) * tq)
        o = t.acc / t.l[:, :tq]
        ms = jnp.mean(o * o, axis=0, keepdims=True)
        y = (o * lax.rsqrt(ms + RMS_EPS) * gain).T
        o_ref[rows, t.lanes] = (y * sa_ref[rows, t.lanes].astype(f32)).astype(o_ref.dtype)

    def interleave(*work):
        for group in itertools.zip_longest(*work):
            for fn in group:
                if fn is not None:
                    fn()

    n_tiles = seq // tq
    n_heads = q_ref.shape[1] // V_DIM

    kabs_max = kmax_ref[...].astype(jnp.bfloat16).astype(f32)
    lane_q = lax.broadcasted_iota(jnp.int32, (2 * tq, V7X_LANES), 1)
    @functools.cache
    def ones_lane0(size):
        lane_k = lax.broadcasted_iota(jnp.int32, (size, V7X_LANES), 1)
        return jnp.where(lane_k == 0, 1.0, 0.0).astype(jnp.bfloat16)

    def shifted_q(t):
        q2 = stacked_q(t)
        bound = jnp.sum(jnp.abs(q2.astype(f32)) * kabs_max[:, t.lanes], axis=-1, keepdims=True)
        neg_shift = jnp.where(lane_q == 0, SOFTMAX_HEADROOM - bound, 0.0)
        return jnp.concatenate([q2, neg_shift.astype(jnp.bfloat16)], axis=1)

    def shifted_exp_block(t, q2a, blk):
        lo, size, masked = blk
        k_aug = jnp.concatenate([k_ref[lo:lo + size, t.lanes], ones_lane0(size)], axis=1)
        s = lax.dot_general(k_aug, q2a, _NT_DIMS, preferred_element_type=f32)
        if masked:
            s = jnp.where(allowed, s, -jnp.inf)
        e = jnp.exp2(s)
        bl = jnp.sum(e, axis=0, keepdims=True)
        t.l = bl if t.l is None else t.l + bl
        t.e.append(e.astype(jnp.bfloat16))

    order = list(range(1, n_tiles, 2)) + list(reversed(range(0, n_tiles, 2)))
    tiles = [Tile(i, hh) for hh in range(n_heads) for i in order]
    for step in range(len(tiles) + 1):
        work = []
        if step < len(tiles):
            t = tiles[step]
            q2a = shifted_q(t)
            work.append([functools.partial(shifted_exp_block, t, q2a, blk) for blk in t.blocks])
        else:
            l_min = functools.reduce(jnp.minimum, [t.l for t in tiles])
            l_max = functools.reduce(jnp.maximum, [t.l for t in tiles])
            in_range = jnp.logical_and(jnp.min(l_min) >= SOFTMAX_L_MIN,
                                       jnp.max(l_max) <= SOFTMAX_L_MAX)
        if step >= 1:
            t = tiles[step - 1]
            work.append([functools.partial(pv_block, t, j) for j in range(len(t.blocks))]
                        + [functools.partial(finish, t)])
        interleave(*work)

    @pl.when(jnp.logical_not(in_range))
    def _():
        for hh, i in itertools.product(range(n_heads), range(n_tiles)):
            t = Tile(i, hh)
            q2 = stacked_q(t)
            for blk in t.blocks:
                score_block(t, q2, blk)
            for j in range(len(t.blocks)):
                exp_block(t, j)
            for j in range(len(t.blocks)):
                pv_block(t, j)
            finish(t)


def _attn_call(lq1, lk1, lq2, lk2, gain, q, k, vt, sa, kmax, lam_init):
    bsz, seq, width = q.shape
    n_heads = width // V_DIM
    hw = ATTN_HEADS_PER_STEP * V_DIM
    head_spec = pl.BlockSpec((None, seq, hw), lambda b, h: (b, 0, h))
    vt_spec = pl.BlockSpec((None, hw, seq), lambda b, h: (b, h, 0))
    kmax_spec = pl.BlockSpec((None, 1, hw), lambda b, h: (b, 0, h))

    def full(shape):
        return pl.BlockSpec(shape, lambda b, h: (0,) * len(shape))

    return pl.pallas_call(
        functools.partial(_attn_kernel, lam_init=lam_init),
        grid=(bsz, n_heads // ATTN_HEADS_PER_STEP),
        in_specs=[full(lq1.shape), full(lk1.shape), full(lq2.shape), full(lk2.shape),
                  full(gain.shape), head_spec, head_spec, vt_spec, head_spec, kmax_spec],
        out_specs=head_spec,
        out_shape=jax.ShapeDtypeStruct((bsz, seq, width), jnp.bfloat16),
        compiler_params=pltpu.CompilerParams(
            dimension_semantics=("arbitrary", "arbitrary"),
            vmem_limit_bytes=V7X_VMEM_LIMIT_BYTES),
        name="diff_attention",
    )(lq1, lk1, lq2, lk2, gain, q, k, vt, sa, kmax)


def _out_kernel(oa_ref, c_ref, g_ref, x_ref, wa_ref, wb_ref, wo_ref,
                lng_ref, lnb_ref, out_ref, *, alpha):
    tm, d = x_ref.shape
    cw = OUT_COLS
    f32 = jnp.float32

    def merge_stage(rows, lo, merged):
        def dots():
            return [jnp.dot(oa_ref[rows, :], wa_ref[:, lo:lo + cw], preferred_element_type=f32),
                    jnp.dot(c_ref[rows, :], wb_ref[:, lo:lo + cw], preferred_element_type=f32)]

        def epilogue(ya, yb):
            merged.append((g_ref[rows, lo:lo + cw].astype(f32) * ya
                           + g_ref[rows, d + lo:d + lo + cw].astype(f32) * yb
                           ).astype(jnp.bfloat16))
        return dots, epilogue

    def out_stage(rows, lo, merged, resid):
        def dots():
            lhs = jnp.concatenate(merged, axis=1)
            return [jnp.dot(lhs, wo_ref[:, lo:lo + cw], preferred_element_type=f32)]

        def epilogue(out):
            resid.append(alpha * x_ref[rows, lo:lo + cw] + out)
            if lo + cw == d:
                mu = sum(jnp.sum(r, axis=-1, keepdims=True) for r in resid) / d
                dev = [r - mu for r in resid]
                var = sum(jnp.sum(v * v, axis=-1, keepdims=True) for v in dev) / d
                inv = lax.rsqrt(var + LN_EPS)
                for j, v in enumerate(dev):
                    cols = slice(j * cw, (j + 1) * cw)
                    out_ref[rows, cols] = v * inv * lng_ref[:, cols] + lnb_ref[:, cols]
        return dots, epilogue

    col_starts = range(0, d, cw)
    sub_tiles = []
    for r0 in range(0, tm, OUT_SUB_ROWS):
        rows = slice(r0, r0 + OUT_SUB_ROWS)
        merged, resid = [], []
        sub_tiles.append(([merge_stage(rows, lo, merged) for lo in col_starts],
                          [out_stage(rows, lo, merged, resid) for lo in col_starts]))
    assert len(sub_tiles) >= 2
    stages = list(sub_tiles[0][0])
    for prev, cur in zip(sub_tiles[:-1], sub_tiles[1:]):
        for m_stage, o_stage in zip(cur[0], prev[1]):
            stages += [m_stage, o_stage]
    stages += sub_tiles[-1][1]
    _run_pipelined(stages)


def _out_call(oa, c, g, x, wa, wb, wo, ln_g, ln_b, alpha):
    bsz, seq, d = x.shape
    tm = OUT_ROWS
    row_spec = pl.BlockSpec((None, tm, d), lambda b, i: (b, i, 0))

    def full(shape, **kwargs):
        return pl.BlockSpec(shape, lambda b, i: (0,) * len(shape), **kwargs)

    resident = dict(pipeline_mode=pl.Buffered(1))
    return pl.pallas_call(
        functools.partial(_out_kernel, alpha=alpha),
        grid=(bsz, seq // tm),
        in_specs=[row_spec, row_spec,
                  pl.BlockSpec((None, tm, 2 * d), lambda b, i: (b, i, 0)),
                  row_spec, full(wa.shape, **resident), full(wb.shape, **resident),
                  full(wo.shape, **resident), full(ln_g.shape), full(ln_b.shape)],
        out_specs=row_spec,
        out_shape=jax.ShapeDtypeStruct((bsz, seq, d), jnp.float32),
        compiler_params=pltpu.CompilerParams(
            dimension_semantics=("arbitrary", "arbitrary"),
            vmem_limit_bytes=V7X_VMEM_LIMIT_BYTES),
        name="merge_out_layernorm",
    )(oa, c, g, x, wa, wb, wo, ln_g, ln_b)


def _rotary_tables(seq_len):
    half = HEAD_DIM // 2
    inv_freq = 1.0 / (ROPE_THETA ** (jnp.arange(half, dtype=jnp.float32) / half))
    pos = jnp.arange(seq_len, dtype=jnp.float32)
    ang = pos[:, None] * inv_freq[None, :]
    cos = jnp.cos(ang)
    sin = jnp.sin(ang)
    reps = V7X_LANES // HEAD_DIM
    cos_t = jnp.tile(jnp.concatenate([cos, cos], -1), (1, reps))
    sin_t = jnp.tile(jnp.concatenate([-sin, sin], -1), (1, reps))
    return cos_t, sin_t


def kernel(x, w_in, b_gate, lambda_q1, lambda_k1, lambda_q2, lambda_k2, subln_g,
           conv_w, conv_b, w_a_out, w_b_out, w_o, ln_g, ln_b):
    depth = w_in.shape[0]
    seq_len = x.shape[1]
    alpha = (2.0 * depth) ** 0.25
    cos_t, sin_t = _rotary_tables(seq_len)
    bf16 = jnp.bfloat16
    for l in range(depth):
        lam_init = 0.8 - 0.6 * math.exp(-0.3 * l)
        q, k, vt, sa, c, g, kmax = _proj_call(
            x, w_in[l].astype(bf16), cos_t, sin_t,
            b_gate[l][None, :], conv_w[l], conv_b[l][None, :])
        oa = _attn_call(lambda_q1[l][None, :], lambda_k1[l][None, :],
                        lambda_q2[l][None, :], lambda_k2[l][None, :],
                        subln_g[l][:, None], q, k, vt, sa, kmax, lam_init)
        x = _out_call(oa, c, g, x, w_a_out[l].astype(bf16), w_b_out[l].astype(bf16),
                      w_o[l].astype(bf16), ln_g[l][None, :], ln_b[l][None, :], alpha)
    return x
```

```python
import functools
import itertools
import math

import jax
import jax.numpy as jnp
from jax import lax
from jax.experimental import pallas as pl
from jax.experimental.pallas import tpu as pltpu

CHUNK = 64
HEAD_DIM = 64
V_DIM = 2 * HEAD_DIM
ROPE_THETA = 10000.0
LN_EPS = 1e-5
RMS_EPS = 1e-5
CONV_K = 3

V7X_LANES = 128
V7X_SUBLANES = 8
V7X_VMEM_LIMIT_BYTES = 56 * 1024 * 1024

PROJ_ROWS = 512
PROJ_COLS = 256
ATTN_ROWS = 256
ATTN_KEYS = 2048
ATTN_HEADS_PER_STEP = 2
SOFTMAX_HEADROOM = 64.0
SOFTMAX_L_MIN = 2.0 ** -80
SOFTMAX_L_MAX = 2.0 ** 100
OUT_ROWS = 1024
OUT_SUB_ROWS = 256
OUT_COLS = 256

_NT_DIMS = (((1,), (1,)), ((), ()))


def _sigmoid(z):
    return 1.0 / (1.0 + jnp.exp(-z))


def _run_pipelined(stages):
    pending = None
    for dots, epilogue in stages:
        acc = dots()
        if pending is not None:
            pending[0](*pending[1])
        pending = (epilogue, acc)
    pending[0](*pending[1])


def _proj_kernel(x_ref, w_ref, cos_ref, sin_ref, bg_ref, cw_ref, cb_ref,
                 q_ref, k_ref, vt_ref, sa_ref, c_ref, g_ref, kmax_ref, ubuf, *, d_model):
    tm = x_ref.shape[0]
    d = d_model
    cw = PROJ_COLS
    xb = x_ref[...].astype(jnp.bfloat16)

    def proj(lo):
        return jnp.dot(xb, w_ref[:, lo:lo + cw], preferred_element_type=jnp.float32)

    cos = cos_ref[...]
    sin_signed = sin_ref[...]
    lane = lax.broadcasted_iota(jnp.int32, (tm, V7X_LANES), 1)
    first_half = (lane & (HEAD_DIM - 1)) < (HEAD_DIM // 2)

    def rotary(t):
        partner = jnp.where(first_half,
                            pltpu.roll(t, V7X_LANES - HEAD_DIM // 2, 1),
                            pltpu.roll(t, HEAD_DIM // 2, 1))
        return t * cos + partner * sin_signed

    scale = HEAD_DIM ** -0.5 * math.log2(math.e)

    def rotary_epilogue(out_ref, lo, mult, absmax_ref=None):
        def epilogue(t):
            for j in range(0, cw, V7X_LANES):
                sl = slice(lo + j, lo + j + V7X_LANES)
                r = rotary(t[:, j:j + V7X_LANES])
                if mult != 1.0:
                    r = r * mult
                out_ref[:, sl] = r.astype(out_ref.dtype)
                if absmax_ref is not None:
                    absmax_ref[:, sl] = jnp.maximum(
                        absmax_ref[:, sl], jnp.max(jnp.abs(r), axis=0, keepdims=True))
        return epilogue

    def v_epilogue(lo):
        def epilogue(t):
            vt_ref[lo:lo + cw, :] = t.T.astype(vt_ref.dtype)
        return epilogue

    def silu_epilogue(lo):
        def epilogue(za):
            sa_ref[:, lo:lo + cw] = (za * _sigmoid(za)).astype(sa_ref.dtype)
        return epilogue

    def conv_in_epilogue(lo):
        def epilogue(h, cgate):
            ubuf[V7X_SUBLANES:V7X_SUBLANES + tm, lo:lo + cw] = cgate * h
        return epilogue

    def conv_out_epilogue(lo):
        sl = slice(lo, lo + cw)

        def epilogue(bgate, zb):
            u = ubuf[V7X_SUBLANES:V7X_SUBLANES + tm, sl]
            u1 = ubuf[V7X_SUBLANES - 1:V7X_SUBLANES - 1 + tm, sl]
            u2 = ubuf[V7X_SUBLANES - 2:V7X_SUBLANES - 2 + tm, sl]
            conv = (cw_ref[0:1, sl] * u2 + cw_ref[1:2, sl] * u1
                    + cw_ref[2:3, sl] * u + cb_ref[:, sl])
            c_ref[:, sl] = (bgate * conv * (zb * _sigmoid(zb))).astype(c_ref.dtype)
            ubuf[0:V7X_SUBLANES, sl] = ubuf[tm:tm + V7X_SUBLANES, sl]
        return epilogue

    def gate_epilogue(lo):
        def epilogue(gl):
            g_ref[:, lo:lo + cw] = _sigmoid(gl + bg_ref[:, lo:lo + cw]).astype(g_ref.dtype)
        return epilogue

    chunks = range(0, d, cw)
    heavy = ([((3 * d + lo,), silu_epilogue(lo)) for lo in chunks]
             + [((8 * d + lo,), gate_epilogue(lo)) for lo in range(0, 2 * d, cw)]
             + [((5 * d + lo, 7 * d + lo), conv_out_epilogue(lo)) for lo in chunks])
    light = [((4 * d + lo, 6 * d + lo), conv_in_epilogue(lo)) for lo in chunks]
    for lo in chunks:
        light.append(((lo,), rotary_epilogue(q_ref, lo, scale)))
        light.append(((d + lo,), rotary_epilogue(k_ref, lo, 1.0, kmax_ref)))
    light += [((2 * d + lo,), v_epilogue(lo)) for lo in chunks]
    assert len(heavy) == len(light)
    stages = [stage for pair in zip(heavy, light) for stage in pair]

    @pl.when(pl.program_id(1) == 0)
    def _():
        ubuf[0:V7X_SUBLANES, :] = jnp.zeros((V7X_SUBLANES, d), jnp.float32)
        kmax_ref[...] = jnp.zeros_like(kmax_ref)

    _run_pipelined([(functools.partial(lambda cols: [proj(lo) for lo in cols], cols), epilogue)
                    for cols, epilogue in stages])


def _proj_call(x, w_bf, cos, sin_signed, b_gate, conv_w, conv_b):
    bsz, seq, d = x.shape
    tm = PROJ_ROWS
    act = jax.ShapeDtypeStruct((bsz, seq, d), jnp.bfloat16)
    row_spec = pl.BlockSpec((None, tm, d), lambda b, i: (b, i, 0))
    tab_spec = pl.BlockSpec((tm, V7X_LANES), lambda b, i: (i, 0))

    def full(shape):
        return pl.BlockSpec(shape, lambda b, i: (0,) * len(shape))

    return pl.pallas_call(
        functools.partial(_proj_kernel, d_model=d),
        grid=(bsz, seq // tm),
        in_specs=[
            row_spec,
            pl.BlockSpec(w_bf.shape, lambda b, i: (0, 0), pipeline_mode=pl.Buffered(1)),
            tab_spec, tab_spec,
            full(b_gate.shape), full(conv_w.shape), full(conv_b.shape),
        ],
        out_specs=[row_spec, row_spec,
                   pl.BlockSpec((None, d, tm), lambda b, i: (b, 0, i)),
                   row_spec, row_spec,
                   pl.BlockSpec((None, tm, 2 * d), lambda b, i: (b, i, 0)),
                   pl.BlockSpec((None, 1, d), lambda b, i: (b, 0, 0))],
        out_shape=[act, act, jax.ShapeDtypeStruct((bsz, d, seq), jnp.bfloat16), act, act,
                   jax.ShapeDtypeStruct((bsz, seq, 2 * d), jnp.bfloat16),
                   jax.ShapeDtypeStruct((bsz, 1, d), jnp.float32)],
        scratch_shapes=[pltpu.VMEM((tm + V7X_SUBLANES, d), jnp.float32)],
        compiler_params=pltpu.CompilerParams(
            dimension_semantics=("arbitrary", "arbitrary"),
            vmem_limit_bytes=V7X_VMEM_LIMIT_BYTES),
        name="proj_rotary_conv_gates",
    )(x, w_bf, cos, sin_signed, b_gate, conv_w, conv_b)


def _attn_kernel(lq1_ref, lk1_ref, lq2_ref, lk2_ref, gain_ref,
                 q_ref, k_ref, vt_ref, sa_ref, kmax_ref, o_ref, *, lam_init):
    seq = q_ref.shape[0]
    tq = ATTN_ROWS
    bk = ATTN_KEYS
    f32 = jnp.float32
    lam = (jnp.exp(jnp.sum(lq1_ref[...] * lk1_ref[...], axis=-1, keepdims=True))
           - jnp.exp(jnp.sum(lq2_ref[...] * lk2_ref[...], axis=-1, keepdims=True))
           + lam_init)

    lane = lax.broadcasted_iota(jnp.int32, (tq, V_DIM), 1)
    map1 = lane < HEAD_DIM
    key = lax.broadcasted_iota(jnp.int32, (tq, 2 * tq), 0)
    col = lax.broadcasted_iota(jnp.int32, (tq, 2 * tq), 1)
    qpos = jnp.where(col >= tq, col - tq, col)
    allowed = (key // CHUNK) <= (qpos // CHUNK)
    gain = gain_ref[...] * (1.0 - lam_init)

    def key_blocks(i):
        blocks = [(lo, min(bk, i * tq - lo), False) for lo in range(0, i * tq, bk)]
        return blocks + [(i * tq, tq, True)]

    def head_lanes(hh):
        return slice(hh * V_DIM, (hh + 1) * V_DIM)

    class Tile:
        def __init__(self, i, hh):
            self.i = i
            self.lanes = head_lanes(hh)
            self.blocks = key_blocks(i)
            self.s, self.e = [], []
            self.m = self.l = self.acc = self.ratio = None

    def stacked_q(t):
        q = q_ref[t.i * tq:(t.i + 1) * tq, t.lanes]
        zero = jnp.zeros_like(q)
        return jnp.concatenate([jnp.where(map1, q, zero), jnp.where(map1, zero, q)], axis=0)

    def score_block(t, q2, blk):
        lo, size, masked = blk
        s = lax.dot_general(k_ref[lo:lo + size, t.lanes], q2, _NT_DIMS,
                            preferred_element_type=f32)
        if masked:
            s = jnp.where(allowed, s, -jnp.inf)
        t.s.append(s)
        bm = jnp.max(s, axis=0, keepdims=True)
        t.m = bm if t.m is None else jnp.maximum(t.m, bm)

    def exp_block(t, j):
        e = jnp.exp2(t.s[j] - t.m)
        bl = jnp.sum(e, axis=0, keepdims=True)
        t.l = bl if t.l is None else t.l + bl
        t.e.append(e.astype(jnp.bfloat16))

    def pv_block(t, j):
        lo, size, _ = t.blocks[j]
        if t.ratio is None:
            t.ratio = (lam * t.l[:, :tq] / t.l[:, tq:]).astype(jnp.bfloat16)
        e = t.e[j]
        a = e[:, :tq] - t.ratio * e[:, tq:]
        pv = jnp.dot(vt_ref[t.lanes, lo:lo + size], a, preferred_element_type=f32)
        t.acc = pv if t.acc is None else t.acc + pv

    def finish(t):
        rows = slice(t.i * tq, (t.i + 1) * tq)
        o = t.acc / t.l[:, :tq]
        ms = jnp.mean(o * o, axis=0, keepdims=True)
        y = (o * lax.rsqrt(ms + RMS_EPS) * gain).T
        o_ref[rows, t.lanes] = (y * sa_ref[rows, t.lanes].astype(f32)).astype(o_ref.dtype)

    def interleave(*work):
        for group in itertools.zip_longest(*work):
            for fn in group:
                if fn is not None:
                    fn()

    n_tiles = seq // tq
    n_heads = q_ref.shape[1] // V_DIM

    kabs_max = kmax_ref[...].astype(jnp.bfloat16).astype(f32)
    lane_q = lax.broadcasted_iota(jnp.int32, (2 * tq, V7X_LANES), 1)
    @functools.cache
    def ones_lane0(size):
        lane_k = lax.broadcasted_iota(jnp.int32, (size, V7X_LANES), 1)
        return jnp.where(lane_k == 0, 1.0, 0.0).astype(jnp.bfloat16)

    def shifted_q(t):
        q2 = stacked_q(t)
        bound = jnp.sum(jnp.abs(q2.astype(f32)) * kabs_max[:, t.lanes], axis=-1, keepdims=True)
        neg_shift = jnp.where(lane_q == 0, SOFTMAX_HEADROOM - bound, 0.0)
        return jnp.concatenate([q2, neg_shift.astype(jnp.bfloat16)], axis=1)

    def shifted_exp_block(t, q2a, blk):
        lo, size, masked = blk
        k_aug = jnp.concatenate([k_ref[lo:lo + size, t.lanes], ones_lane0(size)], axis=1)
        s = lax.dot_general(k_aug, q2a, _NT_DIMS, preferred_element_type=f32)
        if masked:
            s = jnp.where(allowed, s, -jnp.inf)
        e = jnp.exp2(s)
        bl = jnp.sum(e, axis=0, keepdims=True)
        t.l = bl if t.l is None else t.l + bl
        t.e.append(e.astype(jnp.bfloat16))

    order = list(range(1, n_tiles, 2)) + list(reversed(range(0, n_tiles, 2)))
    tiles = [Tile(i, hh) for hh in range(n_heads) for i in order]
    for step in range(len(tiles) + 1):
        work = []
        if step < len(tiles):
            t = tiles[step]
            q2a = shifted_q(t)
            work.append([functools.partial(shifted_exp_block, t, q2a, blk) for blk in t.blocks])
        else:
            l_min = functools.reduce(jnp.minimum, [t.l for t in tiles])
            l_max = functools.reduce(jnp.maximum, [t.l for t in tiles])
            in_range = jnp.logical_and(jnp.min(l_min) >= SOFTMAX_L_MIN,
                                       jnp.max(l_max) <= SOFTMAX_L_MAX)
        if step >= 1:
            t = tiles[step - 1]
            work.append([functools.partial(pv_block, t, j) for j in range(len(t.blocks))]
                        + [functools.partial(finish, t)])
        interleave(*work)

    @pl.when(jnp.logical_not(in_range))
    def _():
        for hh, i in itertools.product(range(n_heads), range(n_tiles)):
            t = Tile(i, hh)
            q2 = stacked_q(t)
            for blk in t.blocks:
                score_block(t, q2, blk)
            for j in range(len(t.blocks)):
                exp_block(t, j)
            for j in range(len(t.blocks)):
                pv_block(t, j)
            finish(t)


def _attn_call(lq1, lk1, lq2, lk2, gain, q, k, vt, sa, kmax, lam_init):
    bsz, seq, width = q.shape
    n_heads = width // V_DIM
    hw = ATTN_HEADS_PER_STEP * V_DIM
    head_spec = pl.BlockSpec((None, seq, hw), lambda b, h: (b, 0, h))
    vt_spec = pl.BlockSpec((None, hw, seq), lambda b, h: (b, h, 0))
    kmax_spec = pl.BlockSpec((None, 1, hw), lambda b, h: (b, 0, h))

    def full(shape):
        return pl.BlockSpec(shape, lambda b, h: (0,) * len(shape))

    return pl.pallas_call(
        functools.partial(_attn_kernel, lam_init=lam_init),
        grid=(bsz, n_heads // ATTN_HEADS_PER_STEP),
        in_specs=[full(lq1.shape), full(lk1.shape), full(lq2.shape), full(lk2.shape),
                  full(gain.shape), head_spec, head_spec, vt_spec, head_spec, kmax_spec],
        out_specs=head_spec,
        out_shape=jax.ShapeDtypeStruct((bsz, seq, width), jnp.bfloat16),
        compiler_params=pltpu.CompilerParams(
            dimension_semantics=("arbitrary", "arbitrary"),
            vmem_limit_bytes=V7X_VMEM_LIMIT_BYTES),
        name="diff_attention",
    )(lq1, lk1, lq2, lk2, gain, q, k, vt, sa, kmax)


def _out_kernel(oa_ref, c_ref, g_ref, x_ref, wa_f32_ref, wb_f32_ref, wo_f32_ref,
                lng_ref, lnb_ref, out_ref, wa_ref, wb_ref, wo_ref, *, alpha):
    tm, d = x_ref.shape
    cw = OUT_COLS
    f32 = jnp.float32

    @pl.when(jnp.logical_and(pl.program_id(0) == 0, pl.program_id(1) == 0))
    def _():
        for src, dst in ((wa_f32_ref, wa_ref), (wb_f32_ref, wb_ref), (wo_f32_ref, wo_ref)):
            for lo in range(0, d, cw):
                dst[:, lo:lo + cw] = src[:, lo:lo + cw].astype(dst.dtype)

    def merge_stage(rows, lo, merged):
        def dots():
            return [jnp.dot(oa_ref[rows, :], wa_ref[:, lo:lo + cw], preferred_element_type=f32),
                    jnp.dot(c_ref[rows, :], wb_ref[:, lo:lo + cw], preferred_element_type=f32)]

        def epilogue(ya, yb):
            merged.append((g_ref[rows, lo:lo + cw].astype(f32) * ya
                           + g_ref[rows, d + lo:d + lo + cw].astype(f32) * yb
                           ).astype(jnp.bfloat16))
        return dots, epilogue

    def out_stage(rows, lo, merged, resid):
        def dots():
            lhs = jnp.concatenate(merged, axis=1)
            return [jnp.dot(lhs, wo_ref[:, lo:lo + cw], preferred_element_type=f32)]

        def epilogue(out):
            resid.append(alpha * x_ref[rows, lo:lo + cw] + out)
            if lo + cw == d:
                mu = sum(jnp.sum(r, axis=-1, keepdims=True) for r in resid) / d
                dev = [r - mu for r in resid]
                var = sum(jnp.sum(v * v, axis=-1, keepdims=True) for v in dev) / d
                inv = lax.rsqrt(var + LN_EPS)
                for j, v in enumerate(dev):
                    cols = slice(j * cw, (j + 1) * cw)
                    out_ref[rows, cols] = v * inv * lng_ref[:, cols] + lnb_ref[:, cols]
        return dots, epilogue

    col_starts = range(0, d, cw)
    sub_tiles = []
    for r0 in range(0, tm, OUT_SUB_ROWS):
        rows = slice(r0, r0 + OUT_SUB_ROWS)
        merged, resid = [], []
        sub_tiles.append(([merge_stage(rows, lo, merged) for lo in col_starts],
                          [out_stage(rows, lo, merged, resid) for lo in col_starts]))
    assert len(sub_tiles) >= 2
    stages = list(sub_tiles[0][0])
    for prev, cur in zip(sub_tiles[:-1], sub_tiles[1:]):
        for m_stage, o_stage in zip(cur[0], prev[1]):
            stages += [m_stage, o_stage]
    stages += sub_tiles[-1][1]
    _run_pipelined(stages)


def _out_call(oa, c, g, x, wa, wb, wo, ln_g, ln_b, alpha):
    bsz, seq, d = x.shape
    tm = OUT_ROWS
    row_spec = pl.BlockSpec((None, tm, d), lambda b, i: (b, i, 0))

    def full(shape, **kwargs):
        return pl.BlockSpec(shape, lambda b, i: (0,) * len(shape), **kwargs)

    resident = dict(pipeline_mode=pl.Buffered(1))
    return pl.pallas_call(
        functools.partial(_out_kernel, alpha=alpha),
        grid=(bsz, seq // tm),
        in_specs=[row_spec, row_spec,
                  pl.BlockSpec((None, tm, 2 * d), lambda b, i: (b, i, 0)),
                  row_spec, full(wa.shape, **resident), full(wb.shape, **resident),
                  full(wo.shape, **resident), full(ln_g.shape), full(ln_b.shape)],
        out_specs=row_spec,
        out_shape=jax.ShapeDtypeStruct((bsz, seq, d), jnp.float32),
        scratch_shapes=[pltpu.VMEM(w.shape, jnp.bfloat16) for w in (wa, wb, wo)],
        compiler_params=pltpu.CompilerParams(
            dimension_semantics=("arbitrary", "arbitrary"),
            vmem_limit_bytes=V7X_VMEM_LIMIT_BYTES),
        name="merge_out_layernorm",
    )(oa, c, g, x, wa, wb, wo, ln_g, ln_b)


def _rotary_tables(seq_len):
    half = HEAD_DIM // 2
    inv_freq = 1.0 / (ROPE_THETA ** (jnp.arange(half, dtype=jnp.float32) / half))
    pos = jnp.arange(seq_len, dtype=jnp.float32)
    ang = pos[:, None] * inv_freq[None, :]
    cos = jnp.cos(ang)
    sin = jnp.sin(ang)
    reps = V7X_LANES // HEAD_DIM
    cos_t = jnp.tile(jnp.concatenate([cos, cos], -1), (1, reps))
    sin_t = jnp.tile(jnp.concatenate([-sin, sin], -1), (1, reps))
    return cos_t, sin_t


def kernel(x, w_in, b_gate, lambda_q1, lambda_k1, lambda_q2, lambda_k2, subln_g,
           conv_w, conv_b, w_a_out, w_b_out, w_o, ln_g, ln_b):
    depth = w_in.shape[0]
    seq_len = x.shape[1]
    alpha = (2.0 * depth) ** 0.25
    cos_t, sin_t = _rotary_tables(seq_len)
    bf16 = jnp.bfloat16
    for l in range(depth):
        lam_init = 0.8 - 0.6 * math.exp(-0.3 * l)
        q, k, vt, sa, c, g, kmax = _proj_call(
            x, w_in[l].astype(bf16), cos_t, sin_t,
            b_gate[l][None, :], conv_w[l], conv_b[l][None, :])
        oa = _attn_call(lambda_q1[l][None, :], lambda_k1[l][None, :],
                        lambda_q2[l][None, :], lambda_k2[l][None, :],
                        subln_g[l][:, None], q, k, vt, sa, kmax, lam_init)
        x = _out_call(oa, c, g, x, w_a_out[l], w_b_out[l], w_o[l],
                      ln_g[l][None, :], ln_b[l][None, :], alpha)
    return x
```

```python
import functools
import itertools
import math

import jax
import jax.numpy as jnp
from jax import lax
from jax.experimental import pallas as pl
from jax.experimental.pallas import tpu as pltpu

CHUNK = 64
HEAD_DIM = 64
V_DIM = 2 * HEAD_DIM
ROPE_THETA = 10000.0
LN_EPS = 1e-5
RMS_EPS = 1e-5
CONV_K = 3

V7X_LANES = 128
V7X_SUBLANES = 8
V7X_VMEM_LIMIT_BYTES = 56 * 1024 * 1024

PROJ_ROWS = 512
PROJ_COLS = 256
PROJ_CAST_COLS = 512
ATTN_ROWS = 256
ATTN_KEYS = 2048
ATTN_HEADS_PER_STEP = 2
SOFTMAX_HEADROOM = 64.0
SOFTMAX_L_MIN = 2.0 ** -80
SOFTMAX_L_MAX = 2.0 ** 100
OUT_ROWS = 1024
OUT_SUB_ROWS = 256
OUT_COLS = 256

_NT_DIMS = (((1,), (1,)), ((), ()))


def _sigmoid(z):
    return 1.0 / (1.0 + jnp.exp(-z))


def _run_pipelined(stages):
    pending = None
    for dots, epilogue in stages:
        acc = dots()
        if pending is not None:
            pending[0](*pending[1])
        pending = (epilogue, acc)
    pending[0](*pending[1])


def _proj_kernel(x_ref, w_hbm, cos_ref, sin_ref, bg_ref, cw_ref, cb_ref,
                 q_ref, k_ref, vt_ref, sa_ref, c_ref, g_ref, kmax_ref,
                 ubuf, w_ref, stage, sem, *, d_model, layer):
    tm = x_ref.shape[0]
    d = d_model
    cw = PROJ_COLS

    @pl.when(jnp.logical_and(pl.program_id(0) == 0, pl.program_id(1) == 0))
    def _():
        cc = stage.shape[2]
        n_chunks = w_ref.shape[1] // cc

        def chunk_copy(c):
            return pltpu.make_async_copy(w_hbm.at[layer, :, pl.ds(c * cc, cc)],
                                         stage.at[c % 2], sem.at[c % 2])

        chunk_copy(0).start()
        for c in range(n_chunks):
            if c + 1 < n_chunks:
                chunk_copy(c + 1).start()
            chunk_copy(c).wait()
            w_ref[:, c * cc:(c + 1) * cc] = stage[c % 2].astype(w_ref.dtype)

    xb = x_ref[...].astype(jnp.bfloat16)

    def proj(lo):
        return jnp.dot(xb, w_ref[:, lo:lo + cw], preferred_element_type=jnp.float32)

    cos = cos_ref[...]
    sin_signed = sin_ref[...]
    lane = lax.broadcasted_iota(jnp.int32, (tm, V7X_LANES), 1)
    first_half = (lane & (HEAD_DIM - 1)) < (HEAD_DIM // 2)

    def rotary(t):
        partner = jnp.where(first_half,
                            pltpu.roll(t, V7X_LANES - HEAD_DIM // 2, 1),
                            pltpu.roll(t, HEAD_DIM // 2, 1))
        return t * cos + partner * sin_signed

    scale = HEAD_DIM ** -0.5 * math.log2(math.e)

    def rotary_epilogue(out_ref, lo, mult, absmax_ref=None):
        def epilogue(t):
            for j in range(0, cw, V7X_LANES):
                sl = slice(lo + j, lo + j + V7X_LANES)
                r = rotary(t[:, j:j + V7X_LANES])
                if mult != 1.0:
                    r = r * mult
                out_ref[:, sl] = r.astype(out_ref.dtype)
                if absmax_ref is not None:
                    absmax_ref[:, sl] = jnp.maximum(
                        absmax_ref[:, sl], jnp.max(jnp.abs(r), axis=0, keepdims=True))
        return epilogue

    def v_epilogue(lo):
        def epilogue(t):
            vt_ref[lo:lo + cw, :] = t.T.astype(vt_ref.dtype)
        return epilogue

    def silu_epilogue(lo):
        def epilogue(za):
            sa_ref[:, lo:lo + cw] = (za * _sigmoid(za)).astype(sa_ref.dtype)
        return epilogue

    def conv_in_epilogue(lo):
        def epilogue(h, cgate):
            ubuf[V7X_SUBLANES:V7X_SUBLANES + tm, lo:lo + cw] = cgate * h
        return epilogue

    def conv_out_epilogue(lo):
        sl = slice(lo, lo + cw)

        def epilogue(bgate, zb):
            u = ubuf[V7X_SUBLANES:V7X_SUBLANES + tm, sl]
            u1 = ubuf[V7X_SUBLANES - 1:V7X_SUBLANES - 1 + tm, sl]
            u2 = ubuf[V7X_SUBLANES - 2:V7X_SUBLANES - 2 + tm, sl]
            conv = (cw_ref[0:1, sl] * u2 + cw_ref[1:2, sl] * u1
                    + cw_ref[2:3, sl] * u + cb_ref[:, sl])
            c_ref[:, sl] = (bgate * conv * (zb * _sigmoid(zb))).astype(c_ref.dtype)
            ubuf[0:V7X_SUBLANES, sl] = ubuf[tm:tm + V7X_SUBLANES, sl]
        return epilogue

    def gate_epilogue(lo):
        def epilogue(gl):
            g_ref[:, lo:lo + cw] = _sigmoid(gl + bg_ref[:, lo:lo + cw]).astype(g_ref.dtype)
        return epilogue

    chunks = range(0, d, cw)
    heavy = ([((3 * d + lo,), silu_epilogue(lo)) for lo in chunks]
             + [((8 * d + lo,), gate_epilogue(lo)) for lo in range(0, 2 * d, cw)]
             + [((5 * d + lo, 7 * d + lo), conv_out_epilogue(lo)) for lo in chunks])
    light = [((4 * d + lo, 6 * d + lo), conv_in_epilogue(lo)) for lo in chunks]
    for lo in chunks:
        light.append(((lo,), rotary_epilogue(q_ref, lo, scale)))
        light.append(((d + lo,), rotary_epilogue(k_ref, lo, 1.0, kmax_ref)))
    light += [((2 * d + lo,), v_epilogue(lo)) for lo in chunks]
    assert len(heavy) == len(light)
    stages = [stage for pair in zip(heavy, light) for stage in pair]

    @pl.when(pl.program_id(1) == 0)
    def _():
        ubuf[0:V7X_SUBLANES, :] = jnp.zeros((V7X_SUBLANES, d), jnp.float32)
        kmax_ref[...] = jnp.zeros_like(kmax_ref)

    _run_pipelined([(functools.partial(lambda cols: [proj(lo) for lo in cols], cols), epilogue)
                    for cols, epilogue in stages])


def _proj_call(x, w_in, cos, sin_signed, b_gate, conv_w, conv_b, layer):
    bsz, seq, d = x.shape
    in_width = w_in.shape[2]
    assert in_width % PROJ_CAST_COLS == 0
    tm = PROJ_ROWS
    act = jax.ShapeDtypeStruct((bsz, seq, d), jnp.bfloat16)
    row_spec = pl.BlockSpec((None, tm, d), lambda b, i: (b, i, 0))
    tab_spec = pl.BlockSpec((tm, V7X_LANES), lambda b, i: (i, 0))

    def full(shape):
        return pl.BlockSpec(shape, lambda b, i: (0,) * len(shape))

    return pl.pallas_call(
        functools.partial(_proj_kernel, d_model=d, layer=layer),
        grid=(bsz, seq // tm),
        in_specs=[
            row_spec,
            pl.BlockSpec(memory_space=pl.ANY),
            tab_spec, tab_spec,
            full(b_gate.shape), full(conv_w.shape), full(conv_b.shape),
        ],
        out_specs=[row_spec, row_spec,
                   pl.BlockSpec((None, d, tm), lambda b, i: (b, 0, i)),
                   row_spec, row_spec,
                   pl.BlockSpec((None, tm, 2 * d), lambda b, i: (b, i, 0)),
                   pl.BlockSpec((None, 1, d), lambda b, i: (b, 0, 0))],
        out_shape=[act, act, jax.ShapeDtypeStruct((bsz, d, seq), jnp.bfloat16), act, act,
                   jax.ShapeDtypeStruct((bsz, seq, 2 * d), jnp.bfloat16),
                   jax.ShapeDtypeStruct((bsz, 1, d), jnp.float32)],
        scratch_shapes=[pltpu.VMEM((tm + V7X_SUBLANES, d), jnp.float32),
                        pltpu.VMEM((d, in_width), jnp.bfloat16),
                        pltpu.VMEM((2, d, PROJ_CAST_COLS), jnp.float32),
                        pltpu.SemaphoreType.DMA((2,))],
        compiler_params=pltpu.CompilerParams(
            dimension_semantics=("arbitrary", "arbitrary"),
            vmem_limit_bytes=V7X_VMEM_LIMIT_BYTES),
        name="proj_rotary_conv_gates",
    )(x, w_in, cos, sin_signed, b_gate, conv_w, conv_b)


def _attn_kernel(lq1_ref, lk1_ref, lq2_ref, lk2_ref, gain_ref,
                 q_ref, k_ref, vt_ref, sa_ref, kmax_ref, o_ref, *, lam_init):
    seq = q_ref.shape[0]
    tq = ATTN_ROWS
    bk = ATTN_KEYS
    f32 = jnp.float32
    lam = (jnp.exp(jnp.sum(lq1_ref[...] * lk1_ref[...], axis=-1, keepdims=True))
           - jnp.exp(jnp.sum(lq2_ref[...] * lk2_ref[...], axis=-1, keepdims=True))
           + lam_init)

    lane = lax.broadcasted_iota(jnp.int32, (tq, V_DIM), 1)
    map1 = lane < HEAD_DIM
    key = lax.broadcasted_iota(jnp.int32, (tq, 2 * tq), 0)
    col = lax.broadcasted_iota(jnp.int32, (tq, 2 * tq), 1)
    qpos = jnp.where(col >= tq, col - tq, col)
    allowed = (key // CHUNK) <= (qpos // CHUNK)
    gain = gain_ref[...] * (1.0 - lam_init)

    def key_blocks(i):
        blocks = [(lo, min(bk, i * tq - lo), False) for lo in range(0, i * tq, bk)]
        return blocks + [(i * tq, tq, True)]

    def head_lanes(hh):
        return slice(hh * V_DIM, (hh + 1) * V_DIM)

    class Tile:
        def __init__(self, i, hh):
            self.i = i
            self.lanes = head_lanes(hh)
            self.blocks = key_blocks(i)
            self.s, self.e = [], []
            self.m = self.l = self.acc = self.ratio = None

    def stacked_q(t):
        q = q_ref[t.i * tq:(t.i + 1) * tq, t.lanes]
        zero = jnp.zeros_like(q)
        return jnp.concatenate([jnp.where(map1, q, zero), jnp.where(map1, zero, q)], axis=0)

    def score_block(t, q2, blk):
        lo, size, masked = blk
        s = lax.dot_general(k_ref[lo:lo + size, t.lanes], q2, _NT_DIMS,
                            preferred_element_type=f32)
        if masked:
            s = jnp.where(allowed, s, -jnp.inf)
        t.s.append(s)
        bm = jnp.max(s, axis=0, keepdims=True)
        t.m = bm if t.m is None else jnp.maximum(t.m, bm)

    def exp_block(t, j):
        e = jnp.exp2(t.s[j] - t.m)
        bl = jnp.sum(e, axis=0, keepdims=True)
        t.l = bl if t.l is None else t.l + bl
        t.e.append(e.astype(jnp.bfloat16))

    def pv_block(t, j):
        lo, size, _ = t.blocks[j]
        if t.ratio is None:
            t.ratio = (lam * t.l[:, :tq] / t.l[:, tq:]).astype(jnp.bfloat16)
        e = t.e[j]
        a = e[:, :tq] - t.ratio * e[:, tq:]
        pv = jnp.dot(vt_ref[t.lanes, lo:lo + size], a, preferred_element_type=f32)
        t.acc = pv if t.acc is None else t.acc + pv

    def finish(t):
        rows = slice(t.i * tq, (t.i + 1) * tq)
        o = t.acc / t.l[:, :tq]
        ms = jnp.mean(o * o, axis=0, keepdims=True)
        y = (o * lax.rsqrt(ms + RMS_EPS) * gain).T
        o_ref[rows, t.lanes] = (y * sa_ref[rows, t.lanes].astype(f32)).astype(o_ref.dtype)

    def interleave(*work):
        for group in itertools.zip_longest(*work):
            for fn in group:
                if fn is not None:
                    fn()

    n_tiles = seq // tq
    n_heads = q_ref.shape[1] // V_DIM

    kabs_max = kmax_ref[...].astype(jnp.bfloat16).astype(f32)
    lane_q = lax.broadcasted_iota(jnp.int32, (2 * tq, V7X_LANES), 1)
    @functools.cache
    def ones_lane0(size):
        lane_k = lax.broadcasted_iota(jnp.int32, (size, V7X_LANES), 1)
        return jnp.where(lane_k == 0, 1.0, 0.0).astype(jnp.bfloat16)

    def shifted_q(t):
        q2 = stacked_q(t)
        bound = jnp.sum(jnp.abs(q2.astype(f32)) * kabs_max[:, t.lanes], axis=-1, keepdims=True)
        neg_shift = jnp.where(lane_q == 0, SOFTMAX_HEADROOM - bound, 0.0)
        return jnp.concatenate([q2, neg_shift.astype(jnp.bfloat16)], axis=1)

    def shifted_exp_block(t, q2a, blk):
        lo, size, masked = blk
        k_aug = jnp.concatenate([k_ref[lo:lo + size, t.lanes], ones_lane0(size)], axis=1)
        s = lax.dot_general(k_aug, q2a, _NT_DIMS, preferred_element_type=f32)
        if masked:
            s = jnp.where(allowed, s, -jnp.inf)
        e = jnp.exp2(s)
        bl = jnp.sum(e, axis=0, keepdims=True)
        t.l = bl if t.l is None else t.l + bl
        t.e.append(e.astype(jnp.bfloat16))

    order = list(range(1, n_tiles, 2)) + list(reversed(range(0, n_tiles, 2)))
    tiles = [Tile(i, hh) for hh in range(n_heads) for i in order]
    for step in range(len(tiles) + 1):
        work = []
        if step < len(tiles):
            t = tiles[step]
            q2a = shifted_q(t)
            work.append([functools.partial(shifted_exp_block, t, q2a, blk) for blk in t.blocks])
        else:
            l_min = functools.reduce(jnp.minimum, [t.l for t in tiles])
            l_max = functools.reduce(jnp.maximum, [t.l for t in tiles])
            in_range = jnp.logical_and(jnp.min(l_min) >= SOFTMAX_L_MIN,
                                       jnp.max(l_max) <= SOFTMAX_L_MAX)
        if step >= 1:
            t = tiles[step - 1]
            work.append([functools.partial(pv_block, t, j) for j in range(len(t.blocks))]
                        + [functools.partial(finish, t)])
        interleave(*work)

    @pl.when(jnp.logical_not(in_range))
    def _():
        for hh, i in itertools.product(range(n_heads), range(n_tiles)):
            t = Tile(i, hh)
            q2 = stacked_q(t)
            for blk in t.blocks:
                score_block(t, q2, blk)
            for j in range(len(t.blocks)):
                exp_block(t, j)
            for j in range(len(t.blocks)):
                pv_block(t, j)
            finish(t)


def _attn_call(lq1, lk1, lq2, lk2, gain, q, k, vt, sa, kmax, lam_init):
    bsz, seq, width = q.shape
    n_heads = width // V_DIM
    hw = ATTN_HEADS_PER_STEP * V_DIM
    head_spec = pl.BlockSpec((None, seq, hw), lambda b, h: (b, 0, h))
    vt_spec = pl.BlockSpec((None, hw, seq), lambda b, h: (b, h, 0))
    kmax_spec = pl.BlockSpec((None, 1, hw), lambda b, h: (b, 0, h))

    def full(shape):
        return pl.BlockSpec(shape, lambda b, h: (0,) * len(shape))

    return pl.pallas_call(
        functools.partial(_attn_kernel, lam_init=lam_init),
        grid=(bsz, n_heads // ATTN_HEADS_PER_STEP),
        in_specs=[full(lq1.shape), full(lk1.shape), full(lq2.shape), full(lk2.shape),
                  full(gain.shape), head_spec, head_spec, vt_spec, head_spec, kmax_spec],
        out_specs=head_spec,
        out_shape=jax.ShapeDtypeStruct((bsz, seq, width), jnp.bfloat16),
        compiler_params=pltpu.CompilerParams(
            dimension_semantics=("arbitrary", "arbitrary"),
            vmem_limit_bytes=V7X_VMEM_LIMIT_BYTES),
        name="diff_attention",
    )(lq1, lk1, lq2, lk2, gain, q, k, vt, sa, kmax)


def _out_kernel(oa_ref, c_ref, g_ref, x_ref, wa_f32_ref, wb_f32_ref, wo_f32_ref,
                lng_ref, lnb_ref, out_ref, wa_ref, wb_ref, wo_ref, *, alpha):
    tm, d = x_ref.shape
    cw = OUT_COLS
    f32 = jnp.float32

    @pl.when(jnp.logical_and(pl.program_id(0) == 0, pl.program_id(1) == 0))
    def _():
        for src, dst in ((wa_f32_ref, wa_ref), (wb_f32_ref, wb_ref), (wo_f32_ref, wo_ref)):
            for lo in range(0, d, cw):
                dst[:, lo:lo + cw] = src[:, lo:lo + cw].astype(dst.dtype)

    def merge_stage(rows, lo, merged):
        def dots():
            return [jnp.dot(oa_ref[rows, :], wa_ref[:, lo:lo + cw], preferred_element_type=f32),
                    jnp.dot(c_ref[rows, :], wb_ref[:, lo:lo + cw], preferred_element_type=f32)]

        def epilogue(ya, yb):
            merged.append((g_ref[rows, lo:lo + cw].astype(f32) * ya
                           + g_ref[rows, d + lo:d + lo + cw].astype(f32) * yb
                           ).astype(jnp.bfloat16))
        return dots, epilogue

    def out_stage(rows, lo, merged, resid):
        def dots():
            lhs = jnp.concatenate(merged, axis=1)
            return [jnp.dot(lhs, wo_ref[:, lo:lo + cw], preferred_element_type=f32)]

        def epilogue(out):
            resid.append(alpha * x_ref[rows, lo:lo + cw] + out)
            if lo + cw == d:
                mu = sum(jnp.sum(r, axis=-1, keepdims=True) for r in resid) / d
                dev = [r - mu for r in resid]
                var = sum(jnp.sum(v * v, axis=-1, keepdims=True) for v in dev) / d
                inv = lax.rsqrt(var + LN_EPS)
                for j, v in enumerate(dev):
                    cols = slice(j * cw, (j + 1) * cw)
                    out_ref[rows, cols] = v * inv * lng_ref[:, cols] + lnb_ref[:, cols]
        return dots, epilogue

    col_starts = range(0, d, cw)
    sub_tiles = []
    for r0 in range(0, tm, OUT_SUB_ROWS):
        rows = slice(r0, r0 + OUT_SUB_ROWS)
        merged, resid = [], []
        sub_tiles.append(([merge_stage(rows, lo, merged) for lo in col_starts],
                          [out_stage(rows, lo, merged, resid) for lo in col_starts]))
    assert len(sub_tiles) >= 2
    stages = list(sub_tiles[0][0])
    for prev, cur in zip(sub_tiles[:-1], sub_tiles[1:]):
        for m_stage, o_stage in zip(cur[0], prev[1]):
            stages += [m_stage, o_stage]
    stages += sub_tiles[-1][1]
    _run_pipelined(stages)


def _out_call(oa, c, g, x, wa, wb, wo, ln_g, ln_b, alpha, layer):
    bsz, seq, d = x.shape
    tm = OUT_ROWS
    row_spec = pl.BlockSpec((None, tm, d), lambda b, i: (b, i, 0))
    weight_spec = pl.BlockSpec((None, d, d), lambda b, i: (layer, 0, 0),
                               pipeline_mode=pl.Buffered(1))

    def full(shape):
        return pl.BlockSpec(shape, lambda b, i: (0,) * len(shape))

    return pl.pallas_call(
        functools.partial(_out_kernel, alpha=alpha),
        grid=(bsz, seq // tm),
        in_specs=[row_spec, row_spec,
                  pl.BlockSpec((None, tm, 2 * d), lambda b, i: (b, i, 0)),
                  row_spec, weight_spec, weight_spec, weight_spec,
                  full(ln_g.shape), full(ln_b.shape)],
        out_specs=row_spec,
        out_shape=jax.ShapeDtypeStruct((bsz, seq, d), jnp.float32),
        scratch_shapes=[pltpu.VMEM((d, d), jnp.bfloat16) for _ in range(3)],
        compiler_params=pltpu.CompilerParams(
            dimension_semantics=("arbitrary", "arbitrary"),
            vmem_limit_bytes=V7X_VMEM_LIMIT_BYTES),
        name="merge_out_layernorm",
    )(oa, c, g, x, wa, wb, wo, ln_g, ln_b)


def _rotary_tables(seq_len):
    half = HEAD_DIM // 2
    inv_freq = 1.0 / (ROPE_THETA ** (jnp.arange(half, dtype=jnp.float32) / half))
    pos = jnp.arange(seq_len, dtype=jnp.float32)
    ang = pos[:, None] * inv_freq[None, :]
    cos = jnp.cos(ang)
    sin = jnp.sin(ang)
    reps = V7X_LANES // HEAD_DIM
    cos_t = jnp.tile(jnp.concatenate([cos, cos], -1), (1, reps))
    sin_t = jnp.tile(jnp.concatenate([-sin, sin], -1), (1, reps))
    return cos_t, sin_t


def kernel(x, w_in, b_gate, lambda_q1, lambda_k1, lambda_q2, lambda_k2, subln_g,
           conv_w, conv_b, w_a_out, w_b_out, w_o, ln_g, ln_b):
    depth = w_in.shape[0]
    seq_len = x.shape[1]
    alpha = (2.0 * depth) ** 0.25
    cos_t, sin_t = _rotary_tables(seq_len)
    for l in range(depth):
        lam_init = 0.8 - 0.6 * math.exp(-0.3 * l)
        q, k, vt, sa, c, g, kmax = _proj_call(
            x, w_in, cos_t, sin_t,
            b_gate[l][None, :], conv_w[l], conv_b[l][None, :], l)
        oa = _attn_call(lambda_q1[l][None, :], lambda_k1[l][None, :],
                        lambda_q2[l][None, :], lambda_k2[l][None, :],
                        subln_g[l][:, None], q, k, vt, sa, kmax, lam_init)
        x = _out_call(oa, c, g, x, w_a_out, w_b_out, w_o,
                      ln_g[l][None, :], ln_b[l][None, :], alpha, l)
    return x
```

```python
import functools
import itertools
import math

import jax
import jax.numpy as jnp
import numpy as np
from jax import lax
from jax.experimental import pallas as pl
from jax.experimental.pallas import tpu as pltpu

CHUNK = 64
HEAD_DIM = 64
V_DIM = 2 * HEAD_DIM
ROPE_THETA = 10000.0
LN_EPS = 1e-5
RMS_EPS = 1e-5
CONV_K = 3

V7X_LANES = 128
V7X_SUBLANES = 8
V7X_VMEM_LIMIT_BYTES = 56 * 1024 * 1024

PROJ_ROWS = 512
PROJ_COLS = 256
PROJ_CAST_COLS = 512
ATTN_ROWS = 256
ATTN_KEYS = 2048
ATTN_HEADS_PER_STEP = 2
SOFTMAX_HEADROOM = 64.0
SOFTMAX_L_MIN = 2.0 ** -80
SOFTMAX_L_MAX = 2.0 ** 100
OUT_ROWS = 1024
OUT_SUB_ROWS = 256
OUT_COLS = 256

_NT_DIMS = (((1,), (1,)), ((), ()))


def _sigmoid(z):
    return 1.0 / (1.0 + jnp.exp(-z))


def _run_pipelined(stages):
    pending = None
    for dots, epilogue in stages:
        acc = dots()
        if pending is not None:
            pending[0](*pending[1])
        pending = (epilogue, acc)
    pending[0](*pending[1])


def _proj_kernel(x_ref, w_hbm, cos_ref, sin_ref, bg_ref, cw_ref, cb_ref,
                 q_ref, k_ref, vt_ref, sa_ref, c_ref, g_ref, kmax_ref,
                 ubuf, w_ref, stage, sem, *, d_model, layer):
    tm = x_ref.shape[0]
    d = d_model
    cw = PROJ_COLS

    @pl.when(jnp.logical_and(pl.program_id(0) == 0, pl.program_id(1) == 0))
    def _():
        cc = stage.shape[2]
        n_chunks = w_ref.shape[1] // cc

        def chunk_copy(c):
            return pltpu.make_async_copy(w_hbm.at[layer, :, pl.ds(c * cc, cc)],
                                         stage.at[c % 2], sem.at[c % 2])

        chunk_copy(0).start()
        for c in range(n_chunks):
            if c + 1 < n_chunks:
                chunk_copy(c + 1).start()
            chunk_copy(c).wait()
            w_ref[:, c * cc:(c + 1) * cc] = stage[c % 2].astype(w_ref.dtype)

    xb = x_ref[...].astype(jnp.bfloat16)

    def proj(lo):
        return jnp.dot(xb, w_ref[:, lo:lo + cw], preferred_element_type=jnp.float32)

    cos = cos_ref[...]
    sin_signed = sin_ref[...]
    lane = lax.broadcasted_iota(jnp.int32, (tm, V7X_LANES), 1)
    first_half = (lane & (HEAD_DIM - 1)) < (HEAD_DIM // 2)

    def rotary(t):
        partner = jnp.where(first_half,
                            pltpu.roll(t, V7X_LANES - HEAD_DIM // 2, 1),
                            pltpu.roll(t, HEAD_DIM // 2, 1))
        return t * cos + partner * sin_signed

    scale = HEAD_DIM ** -0.5 * math.log2(math.e)

    def rotary_epilogue(out_ref, lo, mult, absmax_ref=None):
        def epilogue(t):
            for j in range(0, cw, V7X_LANES):
                sl = slice(lo + j, lo + j + V7X_LANES)
                r = rotary(t[:, j:j + V7X_LANES])
                if mult != 1.0:
                    r = r * mult
                out_ref[:, sl] = r.astype(out_ref.dtype)
                if absmax_ref is not None:
                    absmax_ref[:, sl] = jnp.maximum(
                        absmax_ref[:, sl], jnp.max(jnp.abs(r), axis=0, keepdims=True))
        return epilogue

    def v_epilogue(lo):
        def epilogue(t):
            vt_ref[lo:lo + cw, :] = t.T.astype(vt_ref.dtype)
        return epilogue

    def silu_epilogue(lo):
        def epilogue(za):
            sa_ref[:, lo:lo + cw] = (za * _sigmoid(za)).astype(sa_ref.dtype)
        return epilogue

    def conv_in_epilogue(lo):
        def epilogue(h, cgate):
            ubuf[V7X_SUBLANES:V7X_SUBLANES + tm, lo:lo + cw] = cgate * h
        return epilogue

    def conv_out_epilogue(lo):
        sl = slice(lo, lo + cw)

        def epilogue(bgate, zb):
            u = ubuf[V7X_SUBLANES:V7X_SUBLANES + tm, sl]
            u1 = ubuf[V7X_SUBLANES - 1:V7X_SUBLANES - 1 + tm, sl]
            u2 = ubuf[V7X_SUBLANES - 2:V7X_SUBLANES - 2 + tm, sl]
            conv = (cw_ref[0:1, sl] * u2 + cw_ref[1:2, sl] * u1
                    + cw_ref[2:3, sl] * u + cb_ref[:, sl])
            c_ref[:, sl] = (bgate * conv * (zb * _sigmoid(zb))).astype(c_ref.dtype)
            ubuf[0:V7X_SUBLANES, sl] = ubuf[tm:tm + V7X_SUBLANES, sl]
        return epilogue

    def gate_epilogue(lo):
        def epilogue(gl):
            g_ref[:, lo:lo + cw] = _sigmoid(gl + bg_ref[:, lo:lo + cw]).astype(g_ref.dtype)
        return epilogue

    chunks = range(0, d, cw)
    heavy = ([((3 * d + lo,), silu_epilogue(lo)) for lo in chunks]
             + [((8 * d + lo,), gate_epilogue(lo)) for lo in range(0, 2 * d, cw)]
             + [((5 * d + lo, 7 * d + lo), conv_out_epilogue(lo)) for lo in chunks])
    light = [((4 * d + lo, 6 * d + lo), conv_in_epilogue(lo)) for lo in chunks]
    for lo in chunks:
        light.append(((lo,), rotary_epilogue(q_ref, lo, scale)))
        light.append(((d + lo,), rotary_epilogue(k_ref, lo, 1.0, kmax_ref)))
    light += [((2 * d + lo,), v_epilogue(lo)) for lo in chunks]
    assert len(heavy) == len(light)
    stages = [stage for pair in zip(heavy, light) for stage in pair]

    @pl.when(pl.program_id(1) == 0)
    def _():
        ubuf[0:V7X_SUBLANES, :] = jnp.zeros((V7X_SUBLANES, d), jnp.float32)
        kmax_ref[...] = jnp.zeros_like(kmax_ref)

    _run_pipelined([(functools.partial(lambda cols: [proj(lo) for lo in cols], cols), epilogue)
                    for cols, epilogue in stages])


def _proj_call(x, w_in, cos, sin_signed, b_gate, conv_w, conv_b, layer):
    bsz, seq, d = x.shape
    in_width = w_in.shape[2]
    assert in_width % PROJ_CAST_COLS == 0
    tm = PROJ_ROWS
    act = jax.ShapeDtypeStruct((bsz, seq, d), jnp.bfloat16)
    row_spec = pl.BlockSpec((None, tm, d), lambda b, i: (b, i, 0))
    tab_spec = pl.BlockSpec((tm, V7X_LANES), lambda b, i: (i, 0))

    def full(shape):
        return pl.BlockSpec(shape, lambda b, i: (0,) * len(shape))

    return pl.pallas_call(
        functools.partial(_proj_kernel, d_model=d, layer=layer),
        grid=(bsz, seq // tm),
        in_specs=[
            row_spec,
            pl.BlockSpec(memory_space=pl.ANY),
            tab_spec, tab_spec,
            full(b_gate.shape), full(conv_w.shape), full(conv_b.shape),
        ],
        out_specs=[row_spec, row_spec,
                   pl.BlockSpec((None, d, tm), lambda b, i: (b, 0, i)),
                   row_spec, row_spec,
                   pl.BlockSpec((None, tm, 2 * d), lambda b, i: (b, i, 0)),
                   pl.BlockSpec((None, 1, d), lambda b, i: (b, 0, 0))],
        out_shape=[act, act, jax.ShapeDtypeStruct((bsz, d, seq), jnp.bfloat16), act, act,
                   jax.ShapeDtypeStruct((bsz, seq, 2 * d), jnp.bfloat16),
                   jax.ShapeDtypeStruct((bsz, 1, d), jnp.float32)],
        scratch_shapes=[pltpu.VMEM((tm + V7X_SUBLANES, d), jnp.float32),
                        pltpu.VMEM((d, in_width), jnp.bfloat16),
                        pltpu.VMEM((2, d, PROJ_CAST_COLS), jnp.float32),
                        pltpu.SemaphoreType.DMA((2,))],
        compiler_params=pltpu.CompilerParams(
            dimension_semantics=("arbitrary", "arbitrary"),
            vmem_limit_bytes=V7X_VMEM_LIMIT_BYTES),
        name="proj_rotary_conv_gates",
    )(x, w_in, cos, sin_signed, b_gate, conv_w, conv_b)


def _attn_kernel(lq1_ref, lk1_ref, lq2_ref, lk2_ref, gain_ref,
                 q_ref, k_ref, vt_ref, sa_ref, kmax_ref, o_ref, *, lam_init):
    seq = q_ref.shape[0]
    tq = ATTN_ROWS
    bk = ATTN_KEYS
    f32 = jnp.float32
    lam = (jnp.exp(jnp.sum(lq1_ref[...] * lk1_ref[...], axis=-1, keepdims=True))
           - jnp.exp(jnp.sum(lq2_ref[...] * lk2_ref[...], axis=-1, keepdims=True))
           + lam_init)

    lane = lax.broadcasted_iota(jnp.int32, (tq, V_DIM), 1)
    map1 = lane < HEAD_DIM
    key = lax.broadcasted_iota(jnp.int32, (tq, 2 * tq), 0)
    col = lax.broadcasted_iota(jnp.int32, (tq, 2 * tq), 1)
    qpos = jnp.where(col >= tq, col - tq, col)
    allowed = (key // CHUNK) <= (qpos // CHUNK)
    gain = gain_ref[...] * (1.0 - lam_init)

    def key_blocks(i):
        blocks = [(lo, min(bk, i * tq - lo), False) for lo in range(0, i * tq, bk)]
        return blocks + [(i * tq, tq, True)]

    def head_lanes(hh):
        return slice(hh * V_DIM, (hh + 1) * V_DIM)

    class Tile:
        def __init__(self, i, hh):
            self.i = i
            self.lanes = head_lanes(hh)
            self.blocks = key_blocks(i)
            self.s, self.e = [], []
            self.m = self.l = self.acc = self.ratio = None

    def stacked_q(t):
        q = q_ref[t.i * tq:(t.i + 1) * tq, t.lanes]
        zero = jnp.zeros_like(q)
        return jnp.concatenate([jnp.where(map1, q, zero), jnp.where(map1, zero, q)], axis=0)

    def score_block(t, q2, blk):
        lo, size, masked = blk
        s = lax.dot_general(k_ref[lo:lo + size, t.lanes], q2, _NT_DIMS,
                            preferred_element_type=f32)
        if masked:
            s = jnp.where(allowed, s, -jnp.inf)
        t.s.append(s)
        bm = jnp.max(s, axis=0, keepdims=True)
        t.m = bm if t.m is None else jnp.maximum(t.m, bm)

    def exp_block(t, j):
        e = jnp.exp2(t.s[j] - t.m)
        bl = jnp.sum(e, axis=0, keepdims=True)
        t.l = bl if t.l is None else t.l + bl
        t.e.append(e.astype(jnp.bfloat16))

    def pv_block(t, j):
        lo, size, _ = t.blocks[j]
        if t.ratio is None:
            t.ratio = (lam * t.l[:, :tq] / t.l[:, tq:]).astype(jnp.bfloat16)
        e = t.e[j]
        a = e[:, :tq] - t.ratio * e[:, tq:]
        pv = jnp.dot(vt_ref[t.lanes, lo:lo + size], a, preferred_element_type=f32)
        t.acc = pv if t.acc is None else t.acc + pv

    def finish(t):
        rows = slice(t.i * tq, (t.i + 1) * tq)
        o = t.acc / t.l[:, :tq]
        ms = jnp.mean(o * o, axis=0, keepdims=True)
        y = (o * lax.rsqrt(ms + RMS_EPS)).T * gain
        o_ref[rows, t.lanes] = (y * sa_ref[rows, t.lanes].astype(f32)).astype(o_ref.dtype)

    def interleave(*work):
        for group in itertools.zip_longest(*work):
            for fn in group:
                if fn is not None:
                    fn()

    n_tiles = seq // tq
    n_heads = q_ref.shape[1] // V_DIM

    kabs_max = kmax_ref[...].astype(jnp.bfloat16).astype(f32)
    lane_q = lax.broadcasted_iota(jnp.int32, (2 * tq, V7X_LANES), 1)
    @functools.cache
    def ones_lane0(size):
        lane_k = lax.broadcasted_iota(jnp.int32, (size, V7X_LANES), 1)
        return jnp.where(lane_k == 0, 1.0, 0.0).astype(jnp.bfloat16)

    def shifted_q(t):
        q2 = stacked_q(t)
        bound = jnp.sum(jnp.abs(q2.astype(f32)) * kabs_max[:, t.lanes], axis=-1, keepdims=True)
        neg_shift = jnp.where(lane_q == 0, SOFTMAX_HEADROOM - bound, 0.0)
        return jnp.concatenate([q2, neg_shift.astype(jnp.bfloat16)], axis=1)

    def shifted_exp_block(t, q2a, blk):
        lo, size, masked = blk
        k_aug = jnp.concatenate([k_ref[lo:lo + size, t.lanes], ones_lane0(size)], axis=1)
        s = lax.dot_general(k_aug, q2a, _NT_DIMS, preferred_element_type=f32)
        if masked:
            s = jnp.where(allowed, s, -jnp.inf)
        e = jnp.exp2(s)
        bl = jnp.sum(e, axis=0, keepdims=True)
        t.l = bl if t.l is None else t.l + bl
        t.e.append(e.astype(jnp.bfloat16))

    order = list(range(1, n_tiles, 2)) + list(reversed(range(0, n_tiles, 2)))
    tiles = [Tile(i, hh) for hh in range(n_heads) for i in order]
    for step in range(len(tiles) + 1):
        work = []
        if step < len(tiles):
            t = tiles[step]
            q2a = shifted_q(t)
            work.append([functools.partial(shifted_exp_block, t, q2a, blk) for blk in t.blocks])
        else:
            l_min = functools.reduce(jnp.minimum, [t.l for t in tiles])
            l_max = functools.reduce(jnp.maximum, [t.l for t in tiles])
            in_range = jnp.logical_and(jnp.min(l_min) >= SOFTMAX_L_MIN,
                                       jnp.max(l_max) <= SOFTMAX_L_MAX)
        if step >= 1:
            t = tiles[step - 1]
            work.append([functools.partial(pv_block, t, j) for j in range(len(t.blocks))]
                        + [functools.partial(finish, t)])
        interleave(*work)

    @pl.when(jnp.logical_not(in_range))
    def _():
        for hh, i in itertools.product(range(n_heads), range(n_tiles)):
            t = Tile(i, hh)
            q2 = stacked_q(t)
            for blk in t.blocks:
                score_block(t, q2, blk)
            for j in range(len(t.blocks)):
                exp_block(t, j)
            for j in range(len(t.blocks)):
                pv_block(t, j)
            finish(t)


def _attn_call(lq1, lk1, lq2, lk2, gain, q, k, vt, sa, kmax, lam_init):
    bsz, seq, width = q.shape
    n_heads = width // V_DIM
    hw = ATTN_HEADS_PER_STEP * V_DIM
    head_spec = pl.BlockSpec((None, seq, hw), lambda b, h: (b, 0, h))
    vt_spec = pl.BlockSpec((None, hw, seq), lambda b, h: (b, h, 0))
    kmax_spec = pl.BlockSpec((None, 1, hw), lambda b, h: (b, 0, h))

    def full(shape):
        return pl.BlockSpec(shape, lambda b, h: (0,) * len(shape))

    return pl.pallas_call(
        functools.partial(_attn_kernel, lam_init=lam_init),
        grid=(bsz, n_heads // ATTN_HEADS_PER_STEP),
        in_specs=[full(lq1.shape), full(lk1.shape), full(lq2.shape), full(lk2.shape),
                  full(gain.shape), head_spec, head_spec, vt_spec, head_spec, kmax_spec],
        out_specs=head_spec,
        out_shape=jax.ShapeDtypeStruct((bsz, seq, width), jnp.bfloat16),
        compiler_params=pltpu.CompilerParams(
            dimension_semantics=("arbitrary", "arbitrary"),
            vmem_limit_bytes=V7X_VMEM_LIMIT_BYTES),
        name="diff_attention",
    )(lq1, lk1, lq2, lk2, gain, q, k, vt, sa, kmax)


def _out_kernel(oa_ref, c_ref, g_ref, x_ref, wa_f32_ref, wb_f32_ref, wo_f32_ref,
                lng_ref, lnb_ref, out_ref, wa_ref, wb_ref, wo_ref, *, alpha):
    tm, d = x_ref.shape
    cw = OUT_COLS
    f32 = jnp.float32

    @pl.when(jnp.logical_and(pl.program_id(0) == 0, pl.program_id(1) == 0))
    def _():
        for src, dst in ((wa_f32_ref, wa_ref), (wb_f32_ref, wb_ref), (wo_f32_ref, wo_ref)):
            for lo in range(0, d, cw):
                dst[:, lo:lo + cw] = src[:, lo:lo + cw].astype(dst.dtype)

    def merge_stage(rows, lo, merged):
        def dots():
            return [jnp.dot(oa_ref[rows, :], wa_ref[:, lo:lo + cw], preferred_element_type=f32),
                    jnp.dot(c_ref[rows, :], wb_ref[:, lo:lo + cw], preferred_element_type=f32)]

        def epilogue(ya, yb):
            merged.append((g_ref[rows, lo:lo + cw].astype(f32) * ya
                           + g_ref[rows, d + lo:d + lo + cw].astype(f32) * yb
                           ).astype(jnp.bfloat16))
        return dots, epilogue

    def out_stage(rows, lo, merged, resid):
        def dots():
            lhs = jnp.concatenate(merged, axis=1)
            return [jnp.dot(lhs, wo_ref[:, lo:lo + cw], preferred_element_type=f32)]

        def epilogue(out):
            resid.append(alpha * x_ref[rows, lo:lo + cw] + out)
            if lo + cw == d:
                mu = sum(jnp.sum(r, axis=-1, keepdims=True) for r in resid) / d
                dev = [r - mu for r in resid]
                var = sum(jnp.sum(v * v, axis=-1, keepdims=True) for v in dev) / d
                inv = lax.rsqrt(var + LN_EPS)
                for j, v in enumerate(dev):
                    cols = slice(j * cw, (j + 1) * cw)
                    out_ref[rows, cols] = v * inv * lng_ref[:, cols] + lnb_ref[:, cols]
        return dots, epilogue

    col_starts = range(0, d, cw)
    sub_tiles = []
    for r0 in range(0, tm, OUT_SUB_ROWS):
        rows = slice(r0, r0 + OUT_SUB_ROWS)
        merged, resid = [], []
        sub_tiles.append(([merge_stage(rows, lo, merged) for lo in col_starts],
                          [out_stage(rows, lo, merged, resid) for lo in col_starts]))
    assert len(sub_tiles) >= 2
    stages = list(sub_tiles[0][0])
    for prev, cur in zip(sub_tiles[:-1], sub_tiles[1:]):
        for m_stage, o_stage in zip(cur[0], prev[1]):
            stages += [m_stage, o_stage]
    stages += sub_tiles[-1][1]
    _run_pipelined(stages)


def _out_call(oa, c, g, x, wa, wb, wo, ln_g, ln_b, alpha, layer):
    bsz, seq, d = x.shape
    tm = OUT_ROWS
    row_spec = pl.BlockSpec((None, tm, d), lambda b, i: (b, i, 0))
    weight_spec = pl.BlockSpec((None, d, d), lambda b, i: (layer, 0, 0),
                               pipeline_mode=pl.Buffered(1))

    def full(shape):
        return pl.BlockSpec(shape, lambda b, i: (0,) * len(shape))

    return pl.pallas_call(
        functools.partial(_out_kernel, alpha=alpha),
        grid=(bsz, seq // tm),
        in_specs=[row_spec, row_spec,
                  pl.BlockSpec((None, tm, 2 * d), lambda b, i: (b, i, 0)),
                  row_spec, weight_spec, weight_spec, weight_spec,
                  full(ln_g.shape), full(ln_b.shape)],
        out_specs=row_spec,
        out_shape=jax.ShapeDtypeStruct((bsz, seq, d), jnp.float32),
        scratch_shapes=[pltpu.VMEM((d, d), jnp.bfloat16) for _ in range(3)],
        compiler_params=pltpu.CompilerParams(
            dimension_semantics=("arbitrary", "arbitrary"),
            vmem_limit_bytes=V7X_VMEM_LIMIT_BYTES),
        name="merge_out_layernorm",
    )(oa, c, g, x, wa, wb, wo, ln_g, ln_b)


def _rotary_tables(seq_len):
    half = HEAD_DIM // 2
    inv_freq = 1.0 / (ROPE_THETA ** (np.arange(half, dtype=np.float64) / half))
    ang = np.arange(seq_len, dtype=np.float64)[:, None] * inv_freq[None, :]
    cos = np.cos(ang)
    sin = np.sin(ang)
    reps = V7X_LANES // HEAD_DIM
    cos_t = np.tile(np.concatenate([cos, cos], -1), (1, reps))
    sin_t = np.tile(np.concatenate([-sin, sin], -1), (1, reps))
    return jnp.asarray(cos_t, jnp.float32), jnp.asarray(sin_t, jnp.float32)


def kernel(x, w_in, b_gate, lambda_q1, lambda_k1, lambda_q2, lambda_k2, subln_g,
           conv_w, conv_b, w_a_out, w_b_out, w_o, ln_g, ln_b):
    depth = w_in.shape[0]
    seq_len = x.shape[1]
    alpha = (2.0 * depth) ** 0.25
    cos_t, sin_t = _rotary_tables(seq_len)
    for l in range(depth):
        lam_init = 0.8 - 0.6 * math.exp(-0.3 * l)
        q, k, vt, sa, c, g, kmax = _proj_call(
            x, w_in, cos_t, sin_t,
            b_gate[l][None, :], conv_w[l], conv_b[l][None, :], l)
        oa = _attn_call(lambda_q1[l][None, :], lambda_k1[l][None, :],
                        lambda_q2[l][None, :], lambda_k2[l][None, :],
                        subln_g[l][None, :], q, k, vt, sa, kmax, lam_init)
        x = _out_call(oa, c, g, x, w_a_out, w_b_out, w_o,
                      ln_g[l][None, :], ln_b[l][None, :], alpha, l)
    return x
```

```python
import functools
import itertools
import math

import jax
import jax.numpy as jnp
import numpy as np
from jax import lax
from jax.experimental import pallas as pl
from jax.experimental.pallas import tpu as pltpu

CHUNK = 64
HEAD_DIM = 64
V_DIM = 2 * HEAD_DIM
ROPE_THETA = 10000.0
LN_EPS = 1e-5
RMS_EPS = 1e-5
CONV_K = 3

V7X_LANES = 128
V7X_SUBLANES = 8
V7X_VMEM_LIMIT_BYTES = 56 * 1024 * 1024

PROJ_ROWS = 512
PROJ_COLS = 256
PROJ_CAST_COLS = 512
ATTN_ROWS = 256
ATTN_KEYS = 2048
ATTN_HEADS_PER_STEP = 2
SOFTMAX_HEADROOM = 16.0
SOFTMAX_L_MIN = 2.0 ** -80
SOFTMAX_L_MAX = 2.0 ** 60
OUT_ROWS = 1024
OUT_SUB_ROWS = 256
OUT_COLS = 256

_NT_DIMS = (((1,), (1,)), ((), ()))


def _sigmoid(z):
    return 1.0 / (1.0 + jnp.exp(-z))


def _run_pipelined(stages):
    pending = None
    for dots, epilogue in stages:
        acc = dots()
        if pending is not None:
            pending[0](*pending[1])
        pending = (epilogue, acc)
    pending[0](*pending[1])


def _proj_kernel(x_ref, w_hbm, cos_ref, sin_ref, bg_ref, cw_ref, cb_ref,
                 q_ref, k_ref, vt_ref, sa_ref, c_ref, g_ref, kmax_ref,
                 ubuf, w_ref, stage, sem, *, d_model, layer):
    tm = x_ref.shape[0]
    d = d_model
    cw = PROJ_COLS

    @pl.when(jnp.logical_and(pl.program_id(0) == 0, pl.program_id(1) == 0))
    def _():
        cc = stage.shape[2]
        n_chunks = w_ref.shape[1] // cc

        def chunk_copy(c):
            return pltpu.make_async_copy(w_hbm.at[layer, :, pl.ds(c * cc, cc)],
                                         stage.at[c % 2], sem.at[c % 2])

        chunk_copy(0).start()
        for c in range(n_chunks):
            if c + 1 < n_chunks:
                chunk_copy(c + 1).start()
            chunk_copy(c).wait()
            w_ref[:, c * cc:(c + 1) * cc] = stage[c % 2].astype(w_ref.dtype)

    xb = x_ref[...].astype(jnp.bfloat16)

    def proj(lo):
        return jnp.dot(xb, w_ref[:, lo:lo + cw], preferred_element_type=jnp.float32)

    cos = cos_ref[...]
    sin_signed = sin_ref[...]
    lane = lax.broadcasted_iota(jnp.int32, (tm, V7X_LANES), 1)
    first_half = (lane & (HEAD_DIM - 1)) < (HEAD_DIM // 2)

    def rotary(t):
        partner = jnp.where(first_half,
                            pltpu.roll(t, V7X_LANES - HEAD_DIM // 2, 1),
                            pltpu.roll(t, HEAD_DIM // 2, 1))
        return t * cos + partner * sin_signed

    scale = HEAD_DIM ** -0.5 * math.log2(math.e)

    def rotary_epilogue(out_ref, lo, mult, absmax_ref=None):
        def epilogue(t):
            for j in range(0, cw, V7X_LANES):
                sl = slice(lo + j, lo + j + V7X_LANES)
                r = rotary(t[:, j:j + V7X_LANES])
                if mult != 1.0:
                    r = r * mult
                out_ref[:, sl] = r.astype(out_ref.dtype)
                if absmax_ref is not None:
                    absmax_ref[:, sl] = jnp.maximum(
                        absmax_ref[:, sl], jnp.max(jnp.abs(r), axis=0, keepdims=True))
        return epilogue

    def v_epilogue(lo):
        def epilogue(t):
            vt_ref[lo:lo + cw, :] = t.T.astype(vt_ref.dtype)
        return epilogue

    def silu_epilogue(lo):
        def epilogue(za):
            sa_ref[:, lo:lo + cw] = (za * _sigmoid(za)).astype(sa_ref.dtype)
        return epilogue

    def conv_in_epilogue(lo):
        def epilogue(h, cgate):
            ubuf[V7X_SUBLANES:V7X_SUBLANES + tm, lo:lo + cw] = cgate * h
        return epilogue

    def conv_out_epilogue(lo):
        sl = slice(lo, lo + cw)

        def epilogue(bgate, zb):
            u = ubuf[V7X_SUBLANES:V7X_SUBLANES + tm, sl]
            u1 = ubuf[V7X_SUBLANES - 1:V7X_SUBLANES - 1 + tm, sl]
            u2 = ubuf[V7X_SUBLANES - 2:V7X_SUBLANES - 2 + tm, sl]
            conv = (cw_ref[0:1, sl] * u2 + cw_ref[1:2, sl] * u1
                    + cw_ref[2:3, sl] * u + cb_ref[:, sl])
            c_ref[:, sl] = (bgate * conv * (zb * _sigmoid(zb))).astype(c_ref.dtype)
            ubuf[0:V7X_SUBLANES, sl] = ubuf[tm:tm + V7X_SUBLANES, sl]
        return epilogue

    def gate_epilogue(lo):
        def epilogue(gl):
            g_ref[:, lo:lo + cw] = _sigmoid(gl + bg_ref[:, lo:lo + cw]).astype(g_ref.dtype)
        return epilogue

    chunks = range(0, d, cw)
    heavy = ([((3 * d + lo,), silu_epilogue(lo)) for lo in chunks]
             + [((8 * d + lo,), gate_epilogue(lo)) for lo in range(0, 2 * d, cw)]
             + [((5 * d + lo, 7 * d + lo), conv_out_epilogue(lo)) for lo in chunks])
    light = [((4 * d + lo, 6 * d + lo), conv_in_epilogue(lo)) for lo in chunks]
    for lo in chunks:
        light.append(((lo,), rotary_epilogue(q_ref, lo, scale)))
        light.append(((d + lo,), rotary_epilogue(k_ref, lo, 1.0, kmax_ref)))
    light += [((2 * d + lo,), v_epilogue(lo)) for lo in chunks]
    assert len(heavy) == len(light)
    stages = [stage for pair in zip(heavy, light) for stage in pair]

    @pl.when(pl.program_id(1) == 0)
    def _():
        ubuf[0:V7X_SUBLANES, :] = jnp.zeros((V7X_SUBLANES, d), jnp.float32)
        kmax_ref[...] = jnp.zeros_like(kmax_ref)

    _run_pipelined([(functools.partial(lambda cols: [proj(lo) for lo in cols], cols), epilogue)
                    for cols, epilogue in stages])


def _proj_call(x, w_in, cos, sin_signed, b_gate, conv_w, conv_b, layer):
    bsz, seq, d = x.shape
    in_width = w_in.shape[2]
    assert in_width % PROJ_CAST_COLS == 0
    tm = PROJ_ROWS
    act = jax.ShapeDtypeStruct((bsz, seq, d), jnp.bfloat16)
    row_spec = pl.BlockSpec((None, tm, d), lambda b, i: (b, i, 0))
    tab_spec = pl.BlockSpec((tm, V7X_LANES), lambda b, i: (i, 0))

    def full(shape):
        return pl.BlockSpec(shape, lambda b, i: (0,) * len(shape))

    return pl.pallas_call(
        functools.partial(_proj_kernel, d_model=d, layer=layer),
        grid=(bsz, seq // tm),
        in_specs=[
            row_spec,
            pl.BlockSpec(memory_space=pl.ANY),
            tab_spec, tab_spec,
            full(b_gate.shape), full(conv_w.shape), full(conv_b.shape),
        ],
        out_specs=[row_spec, row_spec,
                   pl.BlockSpec((None, d, tm), lambda b, i: (b, 0, i)),
                   row_spec, row_spec,
                   pl.BlockSpec((None, tm, 2 * d), lambda b, i: (b, i, 0)),
                   pl.BlockSpec((None, 1, d), lambda b, i: (b, 0, 0))],
        out_shape=[act, act, jax.ShapeDtypeStruct((bsz, d, seq), jnp.bfloat16), act, act,
                   jax.ShapeDtypeStruct((bsz, seq, 2 * d), jnp.bfloat16),
                   jax.ShapeDtypeStruct((bsz, 1, d), jnp.float32)],
        scratch_shapes=[pltpu.VMEM((tm + V7X_SUBLANES, d), jnp.float32),
                        pltpu.VMEM((d, in_width), jnp.bfloat16),
                        pltpu.VMEM((2, d, PROJ_CAST_COLS), jnp.float32),
                        pltpu.SemaphoreType.DMA((2,))],
        compiler_params=pltpu.CompilerParams(
            dimension_semantics=("arbitrary", "arbitrary"),
            vmem_limit_bytes=V7X_VMEM_LIMIT_BYTES),
        name="proj_rotary_conv_gates",
    )(x, w_in, cos, sin_signed, b_gate, conv_w, conv_b)


def _attn_kernel(lq1_ref, lk1_ref, lq2_ref, lk2_ref, gain_ref,
                 q_ref, k_ref, vt_ref, sa_ref, kmax_ref, o_ref, *, lam_init):
    seq = q_ref.shape[0]
    tq = ATTN_ROWS
    bk = ATTN_KEYS
    f32 = jnp.float32
    lam = (jnp.exp(jnp.sum(lq1_ref[...] * lk1_ref[...], axis=-1, keepdims=True))
           - jnp.exp(jnp.sum(lq2_ref[...] * lk2_ref[...], axis=-1, keepdims=True))
           + lam_init)

    lane = lax.broadcasted_iota(jnp.int32, (tq, V_DIM), 1)
    map1 = lane < HEAD_DIM
    key = lax.broadcasted_iota(jnp.int32, (tq, 2 * tq), 0)
    col = lax.broadcasted_iota(jnp.int32, (tq, 2 * tq), 1)
    qpos = jnp.where(col >= tq, col - tq, col)
    allowed = (key // CHUNK) <= (qpos // CHUNK)
    gain = gain_ref[...] * (1.0 - lam_init)

    def key_blocks(i):
        blocks = [(lo, min(bk, i * tq - lo), False) for lo in range(0, i * tq, bk)]
        return blocks + [(i * tq, tq, True)]

    def head_lanes(hh):
        return slice(hh * V_DIM, (hh + 1) * V_DIM)

    class Tile:
        def __init__(self, i, hh):
            self.i = i
            self.lanes = head_lanes(hh)
            self.blocks = key_blocks(i)
            self.s, self.e = [], []
            self.m = self.l = self.acc = self.ratio = None

    def stacked_q(t):
        q = q_ref[t.i * tq:(t.i + 1) * tq, t.lanes]
        zero = jnp.zeros_like(q)
        return jnp.concatenate([jnp.where(map1, q, zero), jnp.where(map1, zero, q)], axis=0)

    def score_block(t, q2, blk):
        lo, size, masked = blk
        s = lax.dot_general(k_ref[lo:lo + size, t.lanes], q2, _NT_DIMS,
                            preferred_element_type=f32)
        if masked:
            s = jnp.where(allowed, s, -jnp.inf)
        t.s.append(s)
        bm = jnp.max(s, axis=0, keepdims=True)
        t.m = bm if t.m is None else jnp.maximum(t.m, bm)

    def exp_block(t, j):
        e = jnp.exp2(t.s[j] - t.m)
        bl = jnp.sum(e, axis=0, keepdims=True)
        t.l = bl if t.l is None else t.l + bl
        t.e.append(e.astype(jnp.bfloat16))

    def pv_block(t, j):
        lo, size, _ = t.blocks[j]
        if t.ratio is None:
            t.ratio = (lam * t.l[:, :tq] / t.l[:, tq:]).astype(jnp.bfloat16)
        e = t.e[j]
        a = e[:, :tq] - t.ratio * e[:, tq:]
        pv = jnp.dot(vt_ref[t.lanes, lo:lo + size], a, preferred_element_type=f32)
        t.acc = pv if t.acc is None else t.acc + pv

    def finish(t):
        rows = slice(t.i * tq, (t.i + 1) * tq)
        o = t.acc / t.l[:, :tq]
        ms = jnp.mean(o * o, axis=0, keepdims=True)
        y = (o * lax.rsqrt(ms + RMS_EPS)).T * gain
        o_ref[rows, t.lanes] = (y * sa_ref[rows, t.lanes].astype(f32)).astype(o_ref.dtype)

    def interleave(*work):
        for group in itertools.zip_longest(*work):
            for fn in group:
                if fn is not None:
                    fn()

    n_tiles = seq // tq
    n_heads = q_ref.shape[1] // V_DIM

    kabs_max = kmax_ref[...].astype(jnp.bfloat16).astype(f32)
    lane_q = lax.broadcasted_iota(jnp.int32, (2 * tq, V7X_LANES), 1)
    @functools.cache
    def ones_lane0(size):
        lane_k = lax.broadcasted_iota(jnp.int32, (size, V7X_LANES), 1)
        return jnp.where(lane_k == 0, 1.0, 0.0).astype(jnp.bfloat16)

    def shifted_q(t):
        q2 = stacked_q(t)
        bound = jnp.sum(jnp.abs(q2.astype(f32)) * kabs_max[:, t.lanes], axis=-1, keepdims=True)
        neg_shift = jnp.where(lane_q == 0, SOFTMAX_HEADROOM - bound, 0.0)
        return jnp.concatenate([q2, neg_shift.astype(jnp.bfloat16)], axis=1)

    def shifted_exp_block(t, q2a, blk):
        lo, size, masked = blk
        k_aug = jnp.concatenate([k_ref[lo:lo + size, t.lanes], ones_lane0(size)], axis=1)
        s = lax.dot_general(k_aug, q2a, _NT_DIMS, preferred_element_type=f32)
        if masked:
            s = jnp.where(allowed, s, -jnp.inf)
        e = jnp.exp2(s)
        bl = jnp.sum(e, axis=0, keepdims=True)
        t.l = bl if t.l is None else t.l + bl
        t.e.append(e.astype(jnp.bfloat16))

    order = list(range(1, n_tiles, 2)) + list(reversed(range(0, n_tiles, 2)))
    tiles = [Tile(i, hh) for hh in range(n_heads) for i in order]
    for step in range(len(tiles) + 1):
        work = []
        if step < len(tiles):
            t = tiles[step]
            q2a = shifted_q(t)
            work.append([functools.partial(shifted_exp_block, t, q2a, blk) for blk in t.blocks])
        else:
            l_min = functools.reduce(jnp.minimum, [t.l for t in tiles])
            l_max = functools.reduce(jnp.maximum, [t.l for t in tiles])
            in_range = jnp.logical_and(jnp.min(l_min) >= SOFTMAX_L_MIN,
                                       jnp.max(l_max) <= SOFTMAX_L_MAX)
        if step >= 1:
            t = tiles[step - 1]
            work.append([functools.partial(pv_block, t, j) for j in range(len(t.blocks))]
                        + [functools.partial(finish, t)])
        interleave(*work)

    @pl.when(jnp.logical_not(in_range))
    def _():
        for hh, i in itertools.product(range(n_heads), range(n_tiles)):
            t = Tile(i, hh)
            q2 = stacked_q(t)
            for blk in t.blocks:
                score_block(t, q2, blk)
            for j in range(len(t.blocks)):
                exp_block(t, j)
            for j in range(len(t.blocks)):
                pv_block(t, j)
            finish(t)


def _attn_call(lq1, lk1, lq2, lk2, gain, q, k, vt, sa, kmax, lam_init):
    bsz, seq, width = q.shape
    n_heads = width // V_DIM
    hw = ATTN_HEADS_PER_STEP * V_DIM
    head_spec = pl.BlockSpec((None, seq, hw), lambda b, h: (b, 0, h))
    vt_spec = pl.BlockSpec((None, hw, seq), lambda b, h: (b, h, 0))
    kmax_spec = pl.BlockSpec((None, 1, hw), lambda b, h: (b, 0, h))

    def full(shape):
        return pl.BlockSpec(shape, lambda b, h: (0,) * len(shape))

    return pl.pallas_call(
        functools.partial(_attn_kernel, lam_init=lam_init),
        grid=(bsz, n_heads // ATTN_HEADS_PER_STEP),
        in_specs=[full(lq1.shape), full(lk1.shape), full(lq2.shape), full(lk2.shape),
                  full(gain.shape), head_spec, head_spec, vt_spec, head_spec, kmax_spec],
        out_specs=head_spec,
        out_shape=jax.ShapeDtypeStruct((bsz, seq, width), jnp.bfloat16),
        compiler_params=pltpu.CompilerParams(
            dimension_semantics=("arbitrary", "arbitrary"),
            vmem_limit_bytes=V7X_VMEM_LIMIT_BYTES),
        name="diff_attention",
    )(lq1, lk1, lq2, lk2, gain, q, k, vt, sa, kmax)


def _out_kernel(oa_ref, c_ref, g_ref, x_ref, wa_f32_ref, wb_f32_ref, wo_f32_ref,
                lng_ref, lnb_ref, out_ref, wa_ref, wb_ref, wo_ref, *, alpha):
    tm, d = x_ref.shape
    cw = OUT_COLS
    f32 = jnp.float32

    @pl.when(jnp.logical_and(pl.program_id(0) == 0, pl.program_id(1) == 0))
    def _():
        for src, dst in ((wa_f32_ref, wa_ref), (wb_f32_ref, wb_ref), (wo_f32_ref, wo_ref)):
            for lo in range(0, d, cw):
                dst[:, lo:lo + cw] = src[:, lo:lo + cw].astype(dst.dtype)

    def merge_stage(rows, lo, merged):
        def dots():
            return [jnp.dot(oa_ref[rows, :], wa_ref[:, lo:lo + cw], preferred_element_type=f32),
                    jnp.dot(c_ref[rows, :], wb_ref[:, lo:lo + cw], preferred_element_type=f32)]

        def epilogue(ya, yb):
            merged.append((g_ref[rows, lo:lo + cw].astype(f32) * ya
                           + g_ref[rows, d + lo:d + lo + cw].astype(f32) * yb
                           ).astype(jnp.bfloat16))
        return dots, epilogue

    def out_stage(rows, lo, merged, resid):
        def dots():
            lhs = jnp.concatenate(merged, axis=1)
            return [jnp.dot(lhs, wo_ref[:, lo:lo + cw], preferred_element_type=f32)]

        def epilogue(out):
            resid.append(alpha * x_ref[rows, lo:lo + cw] + out)
            if lo + cw == d:
                mu = sum(jnp.sum(r, axis=-1, keepdims=True) for r in resid) / d
                dev = [r - mu for r in resid]
                var = sum(jnp.sum(v * v, axis=-1, keepdims=True) for v in dev) / d
                inv = lax.rsqrt(var + LN_EPS)
                for j, v in enumerate(dev):
                    cols = slice(j * cw, (j + 1) * cw)
                    out_ref[rows, cols] = v * inv * lng_ref[:, cols] + lnb_ref[:, cols]
        return dots, epilogue

    col_starts = range(0, d, cw)
    sub_tiles = []
    for r0 in range(0, tm, OUT_SUB_ROWS):
        rows = slice(r0, r0 + OUT_SUB_ROWS)
        merged, resid = [], []
        sub_tiles.append(([merge_stage(rows, lo, merged) for lo in col_starts],
                          [out_stage(rows, lo, merged, resid) for lo in col_starts]))
    assert len(sub_tiles) >= 2
    stages = list(sub_tiles[0][0])
    for prev, cur in zip(sub_tiles[:-1], sub_tiles[1:]):
        for m_stage, o_stage in zip(cur[0], prev[1]):
            stages += [m_stage, o_stage]
    stages += sub_tiles[-1][1]
    _run_pipelined(stages)


def _out_call(oa, c, g, x, wa, wb, wo, ln_g, ln_b, alpha, layer):
    bsz, seq, d = x.shape
    tm = OUT_ROWS
    row_spec = pl.BlockSpec((None, tm, d), lambda b, i: (b, i, 0))
    weight_spec = pl.BlockSpec((None, d, d), lambda b, i: (layer, 0, 0),
                               pipeline_mode=pl.Buffered(1))

    def full(shape):
        return pl.BlockSpec(shape, lambda b, i: (0,) * len(shape))

    return pl.pallas_call(
        functools.partial(_out_kernel, alpha=alpha),
        grid=(bsz, seq // tm),
        in_specs=[row_spec, row_spec,
                  pl.BlockSpec((None, tm, 2 * d), lambda b, i: (b, i, 0)),
                  row_spec, weight_spec, weight_spec, weight_spec,
                  full(ln_g.shape), full(ln_b.shape)],
        out_specs=row_spec,
        out_shape=jax.ShapeDtypeStruct((bsz, seq, d), jnp.float32),
        scratch_shapes=[pltpu.VMEM((d, d), jnp.bfloat16) for _ in range(3)],
        compiler_params=pltpu.CompilerParams(
            dimension_semantics=("arbitrary", "arbitrary"),
            vmem_limit_bytes=V7X_VMEM_LIMIT_BYTES),
        name="merge_out_layernorm",
    )(oa, c, g, x, wa, wb, wo, ln_g, ln_b)


def _rotary_tables(seq_len):
    half = HEAD_DIM // 2
    inv_freq = 1.0 / (ROPE_THETA ** (np.arange(half, dtype=np.float64) / half))
    ang = np.arange(seq_len, dtype=np.float64)[:, None] * inv_freq[None, :]
    cos = np.cos(ang)
    sin = np.sin(ang)
    reps = V7X_LANES // HEAD_DIM
    cos_t = np.tile(np.concatenate([cos, cos], -1), (1, reps))
    sin_t = np.tile(np.concatenate([-sin, sin], -1), (1, reps))
    return jnp.asarray(cos_t, jnp.float32), jnp.asarray(sin_t, jnp.float32)


def kernel(x, w_in, b_gate, lambda_q1, lambda_k1, lambda_q2, lambda_k2, subln_g,
           conv_w, conv_b, w_a_out, w_b_out, w_o, ln_g, ln_b):
    depth = w_in.shape[0]
    seq_len = x.shape[1]
    alpha = (2.0 * depth) ** 0.25
    cos_t, sin_t = _rotary_tables(seq_len)
    for l in range(depth):
        lam_init = 0.8 - 0.6 * math.exp(-0.3 * l)
        q, k, vt, sa, c, g, kmax = _proj_call(
            x, w_in, cos_t, sin_t,
            b_gate[l][None, :], conv_w[l], conv_b[l][None, :], l)
        oa = _attn_call(lambda_q1[l][None, :], lambda_k1[l][None, :],
                        lambda_q2[l][None, :], lambda_k2[l][None, :],
                        subln_g[l][None, :], q, k, vt, sa, kmax, lam_init)
        x = _out_call(oa, c, g, x, w_a_out, w_b_out, w_o,
                      ln_g[l][None, :], ln_b[l][None, :], alpha, l)
    return x
```

```python
import functools
import itertools
import math

import jax
import jax.numpy as jnp
import numpy as np
from jax import lax
from jax.experimental import pallas as pl
from jax.experimental.pallas import tpu as pltpu

CHUNK = 64
HEAD_DIM = 64
V_DIM = 2 * HEAD_DIM
ROPE_THETA = 10000.0
LN_EPS = 1e-5
RMS_EPS = 1e-5
CONV_K = 3

V7X_LANES = 128
V7X_SUBLANES = 8
V7X_VMEM_LIMIT_BYTES = 56 * 1024 * 1024

PROJ_ROWS = 512
PROJ_COLS = 256
PROJ_CAST_COLS = 512
ATTN_ROWS = 256
ATTN_KEYS = 2048
ATTN_HEADS_PER_STEP = 2
SOFTMAX_HEADROOM = 16.0
SOFTMAX_L_MIN = 2.0 ** -80
SOFTMAX_L_MAX = 2.0 ** 60
OUT_ROWS = 1024
OUT_SUB_ROWS = 256
OUT_COLS = 256

_NT_DIMS = (((1,), (1,)), ((), ()))


def _sigmoid(z):
    return 1.0 / (1.0 + jnp.exp(-z))


def _run_pipelined(stages):
    pending = None
    for dots, epilogue in stages:
        acc = dots()
        if pending is not None:
            pending[0](*pending[1])
        pending = (epilogue, acc)
    pending[0](*pending[1])


def _proj_kernel(x_ref, w_hbm, cos_ref, sin_ref, bg_ref, cw_ref, cb_ref,
                 q_ref, k_ref, vt_ref, sa_ref, c_ref, g_ref, kmax_ref,
                 ubuf, w_ref, wvt_ref, stage, sem, *, d_model, layer):
    tm = x_ref.shape[0]
    d = d_model
    cw = PROJ_COLS

    @pl.when(jnp.logical_and(pl.program_id(0) == 0, pl.program_id(1) == 0))
    def _():
        cc = stage.shape[2]
        n_chunks = w_ref.shape[1] // cc

        def chunk_copy(c):
            return pltpu.make_async_copy(w_hbm.at[layer, :, pl.ds(c * cc, cc)],
                                         stage.at[c % 2], sem.at[c % 2])

        chunk_copy(0).start()
        for c in range(n_chunks):
            if c + 1 < n_chunks:
                chunk_copy(c + 1).start()
            chunk_copy(c).wait()
            lo = c * cc
            if 2 * d <= lo < 3 * d:
                wvt_ref[lo - 2 * d:lo - 2 * d + cc, :] = stage[c % 2].T.astype(wvt_ref.dtype)
            else:
                w_ref[:, lo:lo + cc] = stage[c % 2].astype(w_ref.dtype)

    xb = x_ref[...].astype(jnp.bfloat16)

    def proj(lo):
        return jnp.dot(xb, w_ref[:, lo:lo + cw], preferred_element_type=jnp.float32)

    cos = cos_ref[...]
    sin_signed = sin_ref[...]
    lane = lax.broadcasted_iota(jnp.int32, (tm, V7X_LANES), 1)
    first_half = (lane & (HEAD_DIM - 1)) < (HEAD_DIM // 2)

    def rotary(t):
        partner = jnp.where(first_half,
                            pltpu.roll(t, V7X_LANES - HEAD_DIM // 2, 1),
                            pltpu.roll(t, HEAD_DIM // 2, 1))
        return t * cos + partner * sin_signed

    scale = HEAD_DIM ** -0.5 * math.log2(math.e)

    def rotary_epilogue(out_ref, lo, mult, absmax_ref=None):
        def epilogue(t):
            for j in range(0, cw, V7X_LANES):
                sl = slice(lo + j, lo + j + V7X_LANES)
                r = rotary(t[:, j:j + V7X_LANES])
                if mult != 1.0:
                    r = r * mult
                out_ref[:, sl] = r.astype(out_ref.dtype)
                if absmax_ref is not None:
                    absmax_ref[:, sl] = jnp.maximum(
                        absmax_ref[:, sl], jnp.max(jnp.abs(r), axis=0, keepdims=True))
        return epilogue

    def vt_stage(lo):
        def dots():
            return [lax.dot_general(wvt_ref[lo:lo + cw, :], xb, _NT_DIMS,
                                    preferred_element_type=jnp.float32)]

        def epilogue(t):
            vt_ref[lo:lo + cw, :] = t.astype(vt_ref.dtype)
        return dots, epilogue

    def silu_epilogue(lo):
        def epilogue(za):
            sa_ref[:, lo:lo + cw] = (za * _sigmoid(za)).astype(sa_ref.dtype)
        return epilogue

    def conv_in_epilogue(lo):
        def epilogue(h, cgate):
            ubuf[V7X_SUBLANES:V7X_SUBLANES + tm, lo:lo + cw] = cgate * h
        return epilogue

    def conv_out_epilogue(lo):
        sl = slice(lo, lo + cw)

        def epilogue(bgate, zb):
            u = ubuf[V7X_SUBLANES:V7X_SUBLANES + tm, sl]
            u1 = ubuf[V7X_SUBLANES - 1:V7X_SUBLANES - 1 + tm, sl]
            u2 = ubuf[V7X_SUBLANES - 2:V7X_SUBLANES - 2 + tm, sl]
            conv = (cw_ref[0:1, sl] * u2 + cw_ref[1:2, sl] * u1
                    + cw_ref[2:3, sl] * u + cb_ref[:, sl])
            c_ref[:, sl] = (bgate * conv * (zb * _sigmoid(zb))).astype(c_ref.dtype)
            ubuf[0:V7X_SUBLANES, sl] = ubuf[tm:tm + V7X_SUBLANES, sl]
        return epilogue

    def gate_epilogue(lo):
        def epilogue(gl):
            g_ref[:, lo:lo + cw] = _sigmoid(gl + bg_ref[:, lo:lo + cw]).astype(g_ref.dtype)
        return epilogue

    chunks = range(0, d, cw)
    def proj_stage(cols, epilogue):
        return (lambda: [proj(lo) for lo in cols]), epilogue

    heavy = ([proj_stage((3 * d + lo,), silu_epilogue(lo)) for lo in chunks]
             + [proj_stage((8 * d + lo,), gate_epilogue(lo)) for lo in range(0, 2 * d, cw)]
             + [proj_stage((5 * d + lo, 7 * d + lo), conv_out_epilogue(lo)) for lo in chunks])
    light = [proj_stage((4 * d + lo, 6 * d + lo), conv_in_epilogue(lo)) for lo in chunks]
    for lo in chunks:
        light.append(proj_stage((lo,), rotary_epilogue(q_ref, lo, scale)))
        light.append(proj_stage((d + lo,), rotary_epilogue(k_ref, lo, 1.0, kmax_ref)))
    light += [vt_stage(lo) for lo in chunks]
    assert len(heavy) == len(light)
    stages = [stage for pair in zip(heavy, light) for stage in pair]

    @pl.when(pl.program_id(1) == 0)
    def _():
        ubuf[0:V7X_SUBLANES, :] = jnp.zeros((V7X_SUBLANES, d), jnp.float32)
        kmax_ref[...] = jnp.zeros_like(kmax_ref)

    _run_pipelined(stages)


def _proj_call(x, w_in, cos, sin_signed, b_gate, conv_w, conv_b, layer):
    bsz, seq, d = x.shape
    in_width = w_in.shape[2]
    assert in_width % PROJ_CAST_COLS == 0
    tm = PROJ_ROWS
    act = jax.ShapeDtypeStruct((bsz, seq, d), jnp.bfloat16)
    row_spec = pl.BlockSpec((None, tm, d), lambda b, i: (b, i, 0))
    tab_spec = pl.BlockSpec((tm, V7X_LANES), lambda b, i: (i, 0))

    def full(shape):
        return pl.BlockSpec(shape, lambda b, i: (0,) * len(shape))

    return pl.pallas_call(
        functools.partial(_proj_kernel, d_model=d, layer=layer),
        grid=(bsz, seq // tm),
        in_specs=[
            row_spec,
            pl.BlockSpec(memory_space=pl.ANY),
            tab_spec, tab_spec,
            full(b_gate.shape), full(conv_w.shape), full(conv_b.shape),
        ],
        out_specs=[row_spec, row_spec,
                   pl.BlockSpec((None, d, tm), lambda b, i: (b, 0, i)),
                   row_spec, row_spec,
                   pl.BlockSpec((None, tm, 2 * d), lambda b, i: (b, i, 0)),
                   pl.BlockSpec((None, 1, d), lambda b, i: (b, 0, 0))],
        out_shape=[act, act, jax.ShapeDtypeStruct((bsz, d, seq), jnp.bfloat16), act, act,
                   jax.ShapeDtypeStruct((bsz, seq, 2 * d), jnp.bfloat16),
                   jax.ShapeDtypeStruct((bsz, 1, d), jnp.float32)],
        scratch_shapes=[pltpu.VMEM((tm + V7X_SUBLANES, d), jnp.float32),
                        pltpu.VMEM((d, in_width), jnp.bfloat16),
                        pltpu.VMEM((d, d), jnp.bfloat16),
                        pltpu.VMEM((2, d, PROJ_CAST_COLS), jnp.float32),
                        pltpu.SemaphoreType.DMA((2,))],
        compiler_params=pltpu.CompilerParams(
            dimension_semantics=("arbitrary", "arbitrary"),
            vmem_limit_bytes=V7X_VMEM_LIMIT_BYTES),
        name="proj_rotary_conv_gates",
    )(x, w_in, cos, sin_signed, b_gate, conv_w, conv_b)


def _attn_kernel(lq1_ref, lk1_ref, lq2_ref, lk2_ref, gain_ref,
                 q_ref, k_ref, vt_ref, sa_ref, kmax_ref, o_ref, *, lam_init):
    seq = q_ref.shape[0]
    tq = ATTN_ROWS
    bk = ATTN_KEYS
    f32 = jnp.float32
    lam = (jnp.exp(jnp.sum(lq1_ref[...] * lk1_ref[...], axis=-1, keepdims=True))
           - jnp.exp(jnp.sum(lq2_ref[...] * lk2_ref[...], axis=-1, keepdims=True))
           + lam_init)

    lane = lax.broadcasted_iota(jnp.int32, (tq, V_DIM), 1)
    map1 = lane < HEAD_DIM
    key = lax.broadcasted_iota(jnp.int32, (tq, 2 * tq), 0)
    col = lax.broadcasted_iota(jnp.int32, (tq, 2 * tq), 1)
    qpos = jnp.where(col >= tq, col - tq, col)
    allowed = (key // CHUNK) <= (qpos // CHUNK)
    gain = gain_ref[...] * (1.0 - lam_init)

    def key_blocks(i):
        blocks = [(lo, min(bk, i * tq - lo), False) for lo in range(0, i * tq, bk)]
        return blocks + [(i * tq, tq, True)]

    def head_lanes(hh):
        return slice(hh * V_DIM, (hh + 1) * V_DIM)

    class Tile:
        def __init__(self, i, hh):
            self.i = i
            self.lanes = head_lanes(hh)
            self.blocks = key_blocks(i)
            self.s, self.e = [], []
            self.m = self.l = self.acc = self.ratio = None

    def stacked_q(t):
        q = q_ref[t.i * tq:(t.i + 1) * tq, t.lanes]
        zero = jnp.zeros_like(q)
        return jnp.concatenate([jnp.where(map1, q, zero), jnp.where(map1, zero, q)], axis=0)

    def score_block(t, q2, blk):
        lo, size, masked = blk
        s = lax.dot_general(k_ref[lo:lo + size, t.lanes], q2, _NT_DIMS,
                            preferred_element_type=f32)
        if masked:
            s = jnp.where(allowed, s, -jnp.inf)
        t.s.append(s)
        bm = jnp.max(s, axis=0, keepdims=True)
        t.m = bm if t.m is None else jnp.maximum(t.m, bm)

    def exp_block(t, j):
        e = jnp.exp2(t.s[j] - t.m)
        bl = jnp.sum(e, axis=0, keepdims=True)
        t.l = bl if t.l is None else t.l + bl
        t.e.append(e.astype(jnp.bfloat16))

    def pv_block(t, j):
        lo, size, _ = t.blocks[j]
        if t.ratio is None:
            t.ratio = (lam * t.l[:, :tq] / t.l[:, tq:]).astype(jnp.bfloat16)
        e = t.e[j]
        a = e[:, :tq] - t.ratio * e[:, tq:]
        pv = jnp.dot(vt_ref[t.lanes, lo:lo + size], a, preferred_element_type=f32)
        t.acc = pv if t.acc is None else t.acc + pv

    def finish(t):
        rows = slice(t.i * tq, (t.i + 1) * tq)
        o = t.acc / t.l[:, :tq]
        ms = jnp.mean(o * o, axis=0, keepdims=True)
        y = (o * lax.rsqrt(ms + RMS_EPS)).T * gain
        o_ref[rows, t.lanes] = (y * sa_ref[rows, t.lanes].astype(f32)).astype(o_ref.dtype)

    def interleave(*work):
        for group in itertools.zip_longest(*work):
            for fn in group:
                if fn is not None:
                    fn()

    n_tiles = seq // tq
    n_heads = q_ref.shape[1] // V_DIM

    kabs_max = kmax_ref[...].astype(jnp.bfloat16).astype(f32)
    lane_q = lax.broadcasted_iota(jnp.int32, (2 * tq, V7X_LANES), 1)
    @functools.cache
    def ones_lane0(size):
        lane_k = lax.broadcasted_iota(jnp.int32, (size, V7X_LANES), 1)
        return jnp.where(lane_k == 0, 1.0, 0.0).astype(jnp.bfloat16)

    def shifted_q(t):
        q2 = stacked_q(t)
        bound = jnp.sum(jnp.abs(q2.astype(f32)) * kabs_max[:, t.lanes], axis=-1, keepdims=True)
        neg_shift = jnp.where(lane_q == 0, SOFTMAX_HEADROOM - bound, 0.0)
        return jnp.concatenate([q2, neg_shift.astype(jnp.bfloat16)], axis=1)

    def shifted_exp_block(t, q2a, blk):
        lo, size, masked = blk
        k_aug = jnp.concatenate([k_ref[lo:lo + size, t.lanes], ones_lane0(size)], axis=1)
        s = lax.dot_general(k_aug, q2a, _NT_DIMS, preferred_element_type=f32)
        if masked:
            s = jnp.where(allowed, s, -jnp.inf)
        e = jnp.exp2(s)
        bl = jnp.sum(e, axis=0, keepdims=True)
        t.l = bl if t.l is None else t.l + bl
        t.e.append(e.astype(jnp.bfloat16))

    order = list(range(1, n_tiles, 2)) + list(reversed(range(0, n_tiles, 2)))
    tiles = [Tile(i, hh) for hh in range(n_heads) for i in order]
    for step in range(len(tiles) + 1):
        work = []
        if step < len(tiles):
            t = tiles[step]
            q2a = shifted_q(t)
            work.append([functools.partial(shifted_exp_block, t, q2a, blk) for blk in t.blocks])
        else:
            l_min = functools.reduce(jnp.minimum, [t.l for t in tiles])
            l_max = functools.reduce(jnp.maximum, [t.l for t in tiles])
            in_range = jnp.logical_and(jnp.min(l_min) >= SOFTMAX_L_MIN,
                                       jnp.max(l_max) <= SOFTMAX_L_MAX)
        if step >= 1:
            t = tiles[step - 1]
            work.append([functools.partial(pv_block, t, j) for j in range(len(t.blocks))]
                        + [functools.partial(finish, t)])
        interleave(*work)

    @pl.when(jnp.logical_not(in_range))
    def _():
        for hh, i in itertools.product(range(n_heads), range(n_tiles)):
            t = Tile(i, hh)
            q2 = stacked_q(t)
            for blk in t.blocks:
                score_block(t, q2, blk)
            for j in range(len(t.blocks)):
                exp_block(t, j)
            for j in range(len(t.blocks)):
                pv_block(t, j)
            finish(t)


def _attn_call(lq1, lk1, lq2, lk2, gain, q, k, vt, sa, kmax, lam_init):
    bsz, seq, width = q.shape
    n_heads = width // V_DIM
    hw = ATTN_HEADS_PER_STEP * V_DIM
    head_spec = pl.BlockSpec((None, seq, hw), lambda b, h: (b, 0, h))
    vt_spec = pl.BlockSpec((None, hw, seq), lambda b, h: (b, h, 0))
    kmax_spec = pl.BlockSpec((None, 1, hw), lambda b, h: (b, 0, h))

    def full(shape):
        return pl.BlockSpec(shape, lambda b, h: (0,) * len(shape))

    return pl.pallas_call(
        functools.partial(_attn_kernel, lam_init=lam_init),
        grid=(bsz, n_heads // ATTN_HEADS_PER_STEP),
        in_specs=[full(lq1.shape), full(lk1.shape), full(lq2.shape), full(lk2.shape),
                  full(gain.shape), head_spec, head_spec, vt_spec, head_spec, kmax_spec],
        out_specs=head_spec,
        out_shape=jax.ShapeDtypeStruct((bsz, seq, width), jnp.bfloat16),
        compiler_params=pltpu.CompilerParams(
            dimension_semantics=("arbitrary", "arbitrary"),
            vmem_limit_bytes=V7X_VMEM_LIMIT_BYTES),
        name="diff_attention",
    )(lq1, lk1, lq2, lk2, gain, q, k, vt, sa, kmax)


def _out_kernel(oa_ref, c_ref, g_ref, x_ref, wa_f32_ref, wb_f32_ref, wo_f32_ref,
                lng_ref, lnb_ref, out_ref, wa_ref, wb_ref, wo_ref, *, alpha):
    tm, d = x_ref.shape
    cw = OUT_COLS
    f32 = jnp.float32

    @pl.when(jnp.logical_and(pl.program_id(0) == 0, pl.program_id(1) == 0))
    def _():
        for src, dst in ((wa_f32_ref, wa_ref), (wb_f32_ref, wb_ref), (wo_f32_ref, wo_ref)):
            for lo in range(0, d, cw):
                dst[:, lo:lo + cw] = src[:, lo:lo + cw].astype(dst.dtype)

    def merge_stage(rows, lo, merged):
        def dots():
            return [jnp.dot(oa_ref[rows, :], wa_ref[:, lo:lo + cw], preferred_element_type=f32),
                    jnp.dot(c_ref[rows, :], wb_ref[:, lo:lo + cw], preferred_element_type=f32)]

        def epilogue(ya, yb):
            merged.append((g_ref[rows, lo:lo + cw].astype(f32) * ya
                           + g_ref[rows, d + lo:d + lo + cw].astype(f32) * yb
                           ).astype(jnp.bfloat16))
        return dots, epilogue

    def out_stage(rows, lo, merged, resid):
        def dots():
            lhs = jnp.concatenate(merged, axis=1)
            return [jnp.dot(lhs, wo_ref[:, lo:lo + cw], preferred_element_type=f32)]

        def epilogue(out):
            resid.append(alpha * x_ref[rows, lo:lo + cw] + out)
            if lo + cw == d:
                mu = sum(jnp.sum(r, axis=-1, keepdims=True) for r in resid) / d
                dev = [r - mu for r in resid]
                var = sum(jnp.sum(v * v, axis=-1, keepdims=True) for v in dev) / d
                inv = lax.rsqrt(var + LN_EPS)
                for j, v in enumerate(dev):
                    cols = slice(j * cw, (j + 1) * cw)
                    out_ref[rows, cols] = v * inv * lng_ref[:, cols] + lnb_ref[:, cols]
        return dots, epilogue

    col_starts = range(0, d, cw)
    sub_tiles = []
    for r0 in range(0, tm, OUT_SUB_ROWS):
        rows = slice(r0, r0 + OUT_SUB_ROWS)
        merged, resid = [], []
        sub_tiles.append(([merge_stage(rows, lo, merged) for lo in col_starts],
                          [out_stage(rows, lo, merged, resid) for lo in col_starts]))
    assert len(sub_tiles) >= 2
    stages = list(sub_tiles[0][0])
    for prev, cur in zip(sub_tiles[:-1], sub_tiles[1:]):
        for m_stage, o_stage in zip(cur[0], prev[1]):
            stages += [m_stage, o_stage]
    stages += sub_tiles[-1][1]
    _run_pipelined(stages)


def _out_call(oa, c, g, x, wa, wb, wo, ln_g, ln_b, alpha, layer):
    bsz, seq, d = x.shape
    tm = OUT_ROWS
    row_spec = pl.BlockSpec((None, tm, d), lambda b, i: (b, i, 0))
    weight_spec = pl.BlockSpec((None, d, d), lambda b, i: (layer, 0, 0),
                               pipeline_mode=pl.Buffered(1))

    def full(shape):
        return pl.BlockSpec(shape, lambda b, i: (0,) * len(shape))

    return pl.pallas_call(
        functools.partial(_out_kernel, alpha=alpha),
        grid=(bsz, seq // tm),
        in_specs=[row_spec, row_spec,
                  pl.BlockSpec((None, tm, 2 * d), lambda b, i: (b, i, 0)),
                  row_spec, weight_spec, weight_spec, weight_spec,
                  full(ln_g.shape), full(ln_b.shape)],
        out_specs=row_spec,
        out_shape=jax.ShapeDtypeStruct((bsz, seq, d), jnp.float32),
        scratch_shapes=[pltpu.VMEM((d, d), jnp.bfloat16) for _ in range(3)],
        compiler_params=pltpu.CompilerParams(
            dimension_semantics=("arbitrary", "arbitrary"),
            vmem_limit_bytes=V7X_VMEM_LIMIT_BYTES),
        name="merge_out_layernorm",
    )(oa, c, g, x, wa, wb, wo, ln_g, ln_b)


def _rotary_tables(seq_len):
    half = HEAD_DIM // 2
    inv_freq = 1.0 / (ROPE_THETA ** (np.arange(half, dtype=np.float64) / half))
    ang = np.arange(seq_len, dtype=np.float64)[:, None] * inv_freq[None, :]
    cos = np.cos(ang)
    sin = np.sin(ang)
    reps = V7X_LANES // HEAD_DIM
    cos_t = np.tile(np.concatenate([cos, cos], -1), (1, reps))
    sin_t = np.tile(np.concatenate([-sin, sin], -1), (1, reps))
    return jnp.asarray(cos_t, jnp.float32), jnp.asarray(sin_t, jnp.float32)


def kernel(x, w_in, b_gate, lambda_q1, lambda_k1, lambda_q2, lambda_k2, subln_g,
           conv_w, conv_b, w_a_out, w_b_out, w_o, ln_g, ln_b):
    depth = w_in.shape[0]
    seq_len = x.shape[1]
    alpha = (2.0 * depth) ** 0.25
    cos_t, sin_t = _rotary_tables(seq_len)
    for l in range(depth):
        lam_init = 0.8 - 0.6 * math.exp(-0.3 * l)
        q, k, vt, sa, c, g, kmax = _proj_call(
            x, w_in, cos_t, sin_t,
            b_gate[l][None, :], conv_w[l], conv_b[l][None, :], l)
        oa = _attn_call(lambda_q1[l][None, :], lambda_k1[l][None, :],
                        lambda_q2[l][None, :], lambda_k2[l][None, :],
                        subln_g[l][None, :], q, k, vt, sa, kmax, lam_init)
        x = _out_call(oa, c, g, x, w_a_out, w_b_out, w_o,
                      ln_g[l][None, :], ln_b[l][None, :], alpha, l)
    return x
```

```python
import functools
import itertools
import math

import jax
import jax.numpy as jnp
import numpy as np
from jax import lax
from jax.experimental import pallas as pl
from jax.experimental.pallas import tpu as pltpu

CHUNK = 64
HEAD_DIM = 64
V_DIM = 2 * HEAD_DIM
ROPE_THETA = 10000.0
LN_EPS = 1e-5
RMS_EPS = 1e-5

V7X_LANES = 128
V7X_SUBLANES = 8
V7X_VMEM_LIMIT_BYTES = 56 * 1024 * 1024

PROJ_ROWS = 512
PROJ_COLS = 256
PROJ_CAST_COLS = 512
ATTN_ROWS = 256
ATTN_KEYS = 2048
ATTN_HEADS_PER_STEP = 2
SOFTMAX_HEADROOM = 16.0
SOFTMAX_L_MIN = 2.0 ** -80
SOFTMAX_L_MAX = 2.0 ** 60
OUT_ROWS = 1024
OUT_SUB_ROWS = 256
OUT_COLS = 256

_NT_DIMS = (((1,), (1,)), ((), ()))


def _sigmoid(z):
    return 1.0 / (1.0 + jnp.exp(-z))


def _run_pipelined(stages):
    pending = None
    for dots, epilogue in stages:
        acc = dots()
        if pending is not None:
            pending[0](*pending[1])
        pending = (epilogue, acc)
    pending[0](*pending[1])


def _proj_kernel(x_ref, w_hbm, cos_ref, sin_ref, bg_ref, cw_ref, cb_ref,
                 q_ref, k_ref, vt_ref, sa_ref, c_ref, g_ref, kmax_ref,
                 ubuf, w_ref, wvt_ref, stage, sem, *, d_model, layer):
    tm = x_ref.shape[0]
    d = d_model
    cw = PROJ_COLS

    @pl.when(jnp.logical_and(pl.program_id(0) == 0, pl.program_id(1) == 0))
    def _():
        cc = stage.shape[2]
        n_chunks = w_ref.shape[1] // cc

        def chunk_copy(c):
            return pltpu.make_async_copy(w_hbm.at[layer, :, pl.ds(c * cc, cc)],
                                         stage.at[c % 2], sem.at[c % 2])

        chunk_copy(0).start()
        for c in range(n_chunks):
            if c + 1 < n_chunks:
                chunk_copy(c + 1).start()
            chunk_copy(c).wait()
            lo = c * cc
            if 2 * d <= lo < 3 * d:
                wvt_ref[lo - 2 * d:lo - 2 * d + cc, :] = stage[c % 2].T.astype(wvt_ref.dtype)
            else:
                w_ref[:, lo:lo + cc] = stage[c % 2].astype(w_ref.dtype)

    xb = x_ref[...].astype(jnp.bfloat16)

    def proj(lo):
        return jnp.dot(xb, w_ref[:, lo:lo + cw], preferred_element_type=jnp.float32)

    cos = cos_ref[...]
    sin_signed = sin_ref[...]
    lane = lax.broadcasted_iota(jnp.int32, (tm, V7X_LANES), 1)
    first_half = (lane & (HEAD_DIM - 1)) < (HEAD_DIM // 2)

    def rotary(t):
        partner = jnp.where(first_half,
                            pltpu.roll(t, V7X_LANES - HEAD_DIM // 2, 1),
                            pltpu.roll(t, HEAD_DIM // 2, 1))
        return t * cos + partner * sin_signed

    scale = HEAD_DIM ** -0.5 * math.log2(math.e)

    def rotary_epilogue(out_ref, lo, mult, absmax_ref=None):
        def epilogue(t):
            for j in range(0, cw, V7X_LANES):
                sl = slice(lo + j, lo + j + V7X_LANES)
                r = rotary(t[:, j:j + V7X_LANES])
                if mult != 1.0:
                    r = r * mult
                out_ref[:, sl] = r.astype(out_ref.dtype)
                if absmax_ref is not None:
                    absmax_ref[:, sl] = jnp.maximum(
                        absmax_ref[:, sl], jnp.max(jnp.abs(r), axis=0, keepdims=True))
        return epilogue

    def vt_stage(lo):
        def dots():
            return [lax.dot_general(wvt_ref[lo:lo + cw, :], xb, _NT_DIMS,
                                    preferred_element_type=jnp.float32)]

        def epilogue(t):
            vt_ref[lo:lo + cw, :] = t.astype(vt_ref.dtype)
        return dots, epilogue

    def silu_epilogue(lo):
        def epilogue(za):
            sa_ref[:, lo:lo + cw] = (za * _sigmoid(za)).astype(sa_ref.dtype)
        return epilogue

    def conv_in_epilogue(lo):
        def epilogue(h, cgate):
            ubuf[V7X_SUBLANES:V7X_SUBLANES + tm, lo:lo + cw] = cgate * h
        return epilogue

    def conv_out_epilogue(lo):
        sl = slice(lo, lo + cw)

        def epilogue(bgate, zb):
            u = ubuf[V7X_SUBLANES:V7X_SUBLANES + tm, sl]
            u1 = ubuf[V7X_SUBLANES - 1:V7X_SUBLANES - 1 + tm, sl]
            u2 = ubuf[V7X_SUBLANES - 2:V7X_SUBLANES - 2 + tm, sl]
            conv = (cw_ref[0:1, sl] * u2 + cw_ref[1:2, sl] * u1
                    + cw_ref[2:3, sl] * u + cb_ref[:, sl])
            c_ref[:, sl] = (bgate * conv * (zb * _sigmoid(zb))).astype(c_ref.dtype)
            ubuf[0:V7X_SUBLANES, sl] = ubuf[tm:tm + V7X_SUBLANES, sl]
        return epilogue

    def gate_epilogue(lo):
        def epilogue(gl):
            g_ref[:, lo:lo + cw] = _sigmoid(gl + bg_ref[:, lo:lo + cw]).astype(g_ref.dtype)
        return epilogue

    chunks = range(0, d, cw)
    def proj_stage(cols, epilogue):
        return (lambda: [proj(lo) for lo in cols]), epilogue

    heavy = ([proj_stage((3 * d + lo,), silu_epilogue(lo)) for lo in chunks]
             + [proj_stage((8 * d + lo,), gate_epilogue(lo)) for lo in range(0, 2 * d, cw)]
             + [proj_stage((5 * d + lo, 7 * d + lo), conv_out_epilogue(lo)) for lo in chunks])
    light = [proj_stage((4 * d + lo, 6 * d + lo), conv_in_epilogue(lo)) for lo in chunks]
    for lo in chunks:
        light.append(proj_stage((lo,), rotary_epilogue(q_ref, lo, scale)))
        light.append(proj_stage((d + lo,), rotary_epilogue(k_ref, lo, 1.0, kmax_ref)))
    light += [vt_stage(lo) for lo in chunks]
    assert len(heavy) == len(light)
    stages = [stage for pair in zip(heavy, light) for stage in pair]

    @pl.when(pl.program_id(1) == 0)
    def _():
        ubuf[0:V7X_SUBLANES, :] = jnp.zeros((V7X_SUBLANES, d), jnp.float32)
        kmax_ref[...] = jnp.zeros_like(kmax_ref)

    _run_pipelined(stages)


def _proj_call(x, w_in, cos, sin_signed, b_gate, conv_w, conv_b, layer):
    bsz, seq, d = x.shape
    in_width = w_in.shape[2]
    assert in_width % PROJ_CAST_COLS == 0
    tm = PROJ_ROWS
    act = jax.ShapeDtypeStruct((bsz, seq, d), jnp.bfloat16)
    row_spec = pl.BlockSpec((None, tm, d), lambda b, i: (b, i, 0))
    tab_spec = pl.BlockSpec((tm, V7X_LANES), lambda b, i: (i, 0))

    def full(shape):
        return pl.BlockSpec(shape, lambda b, i: (0,) * len(shape))

    return pl.pallas_call(
        functools.partial(_proj_kernel, d_model=d, layer=layer),
        grid=(bsz, seq // tm),
        in_specs=[
            row_spec,
            pl.BlockSpec(memory_space=pl.ANY),
            tab_spec, tab_spec,
            full(b_gate.shape), full(conv_w.shape), full(conv_b.shape),
        ],
        out_specs=[row_spec, row_spec,
                   pl.BlockSpec((None, d, tm), lambda b, i: (b, 0, i)),
                   row_spec, row_spec,
                   pl.BlockSpec((None, tm, 2 * d), lambda b, i: (b, i, 0)),
                   pl.BlockSpec((None, 1, d), lambda b, i: (b, 0, 0))],
        out_shape=[act, act, jax.ShapeDtypeStruct((bsz, d, seq), jnp.bfloat16), act, act,
                   jax.ShapeDtypeStruct((bsz, seq, 2 * d), jnp.bfloat16),
                   jax.ShapeDtypeStruct((bsz, 1, d), jnp.float32)],
        scratch_shapes=[pltpu.VMEM((tm + V7X_SUBLANES, d), jnp.float32),
                        pltpu.VMEM((d, in_width), jnp.bfloat16),
                        pltpu.VMEM((d, d), jnp.bfloat16),
                        pltpu.VMEM((2, d, PROJ_CAST_COLS), jnp.float32),
                        pltpu.SemaphoreType.DMA((2,))],
        compiler_params=pltpu.CompilerParams(
            dimension_semantics=("arbitrary", "arbitrary"),
            vmem_limit_bytes=V7X_VMEM_LIMIT_BYTES),
        name="proj_rotary_conv_gates",
    )(x, w_in, cos, sin_signed, b_gate, conv_w, conv_b)


def _attn_kernel(lq1_ref, lk1_ref, lq2_ref, lk2_ref, gain_ref,
                 q_ref, k_ref, vt_ref, sa_ref, kmax_ref, o_ref, *, lam_init):
    seq = q_ref.shape[0]
    tq = ATTN_ROWS
    bk = ATTN_KEYS
    f32 = jnp.float32
    lam = (jnp.exp(jnp.sum(lq1_ref[...] * lk1_ref[...], axis=-1, keepdims=True))
           - jnp.exp(jnp.sum(lq2_ref[...] * lk2_ref[...], axis=-1, keepdims=True))
           + lam_init)

    lane = lax.broadcasted_iota(jnp.int32, (tq, V_DIM), 1)
    map1 = lane < HEAD_DIM
    key = lax.broadcasted_iota(jnp.int32, (tq, 2 * tq), 0)
    col = lax.broadcasted_iota(jnp.int32, (tq, 2 * tq), 1)
    qpos = jnp.where(col >= tq, col - tq, col)
    allowed = (key // CHUNK) <= (qpos // CHUNK)
    gain = gain_ref[...] * (1.0 - lam_init)

    def key_blocks(i):
        blocks = [(lo, min(bk, i * tq - lo), False) for lo in range(0, i * tq, bk)]
        return blocks + [(i * tq, tq, True)]

    def head_lanes(hh):
        return slice(hh * V_DIM, (hh + 1) * V_DIM)

    class Tile:
        def __init__(self, i, hh):
            self.i = i
            self.lanes = head_lanes(hh)
            self.blocks = key_blocks(i)
            self.s, self.e = [], []
            self.m = self.l = self.acc = self.ratio = None

    def stacked_q(t):
        q = q_ref[t.i * tq:(t.i + 1) * tq, t.lanes]
        zero = jnp.zeros_like(q)
        return jnp.concatenate([jnp.where(map1, q, zero), jnp.where(map1, zero, q)], axis=0)

    def score_block(t, q2, blk):
        lo, size, masked = blk
        s = lax.dot_general(k_ref[lo:lo + size, t.lanes], q2, _NT_DIMS,
                            preferred_element_type=f32)
        if masked:
            s = jnp.where(allowed, s, -jnp.inf)
        t.s.append(s)
        bm = jnp.max(s, axis=0, keepdims=True)
        t.m = bm if t.m is None else jnp.maximum(t.m, bm)

    def exp_block(t, j):
        e = jnp.exp2(t.s[j] - t.m)
        bl = jnp.sum(e, axis=0, keepdims=True)
        t.l = bl if t.l is None else t.l + bl
        t.e.append(e.astype(jnp.bfloat16))

    def pv_block(t, j):
        lo, size, _ = t.blocks[j]
        if t.ratio is None:
            t.ratio = (lam * t.l[:, :tq] / t.l[:, tq:]).astype(jnp.bfloat16)
        e = t.e[j]
        a = e[:, :tq] - t.ratio * e[:, tq:]
        pv = jnp.dot(vt_ref[t.lanes, lo:lo + size], a, preferred_element_type=f32)
        t.acc = pv if t.acc is None else t.acc + pv

    def finish(t):
        rows = slice(t.i * tq, (t.i + 1) * tq)
        o = t.acc / t.l[:, :tq]
        ms = jnp.mean(o * o, axis=0, keepdims=True)
        y = (o * lax.rsqrt(ms + RMS_EPS)).T * gain
        o_ref[rows, t.lanes] = (y * sa_ref[rows, t.lanes].astype(f32)).astype(o_ref.dtype)

    def interleave(*work):
        for group in itertools.zip_longest(*work):
            for fn in group:
                if fn is not None:
                    fn()

    n_tiles = seq // tq
    n_heads = q_ref.shape[1] // V_DIM

    kabs_max = kmax_ref[...].astype(jnp.bfloat16).astype(f32)
    lane_q = lax.broadcasted_iota(jnp.int32, (2 * tq, V7X_LANES), 1)

    @functools.cache
    def ones_lane0(size):
        lane_k = lax.broadcasted_iota(jnp.int32, (size, V7X_LANES), 1)
        return jnp.where(lane_k == 0, 1.0, 0.0).astype(jnp.bfloat16)

    def shifted_q(t):
        q2 = stacked_q(t)
        bound = jnp.sum(jnp.abs(q2.astype(f32)) * kabs_max[:, t.lanes], axis=-1, keepdims=True)
        neg_shift = jnp.where(lane_q == 0, SOFTMAX_HEADROOM - bound, 0.0)
        return jnp.concatenate([q2, neg_shift.astype(jnp.bfloat16)], axis=1)

    def shifted_exp_block(t, q2a, blk):
        lo, size, masked = blk
        k_aug = jnp.concatenate([k_ref[lo:lo + size, t.lanes], ones_lane0(size)], axis=1)
        s = lax.dot_general(k_aug, q2a, _NT_DIMS, preferred_element_type=f32)
        if masked:
            s = jnp.where(allowed, s, -jnp.inf)
        e = jnp.exp2(s)
        bl = jnp.sum(e, axis=0, keepdims=True)
        t.l = bl if t.l is None else t.l + bl
        t.e.append(e.astype(jnp.bfloat16))

    order = list(range(1, n_tiles, 2)) + list(reversed(range(0, n_tiles, 2)))
    tiles = [Tile(i, hh) for hh in range(n_heads) for i in order]
    for step in range(len(tiles) + 1):
        work = []
        if step < len(tiles):
            t = tiles[step]
            q2a = shifted_q(t)
            work.append([functools.partial(shifted_exp_block, t, q2a, blk) for blk in t.blocks])
        else:
            l_min = functools.reduce(jnp.minimum, [t.l for t in tiles])
            l_max = functools.reduce(jnp.maximum, [t.l for t in tiles])
            in_range = jnp.logical_and(jnp.min(l_min) >= SOFTMAX_L_MIN,
                                       jnp.max(l_max) <= SOFTMAX_L_MAX)
        if step >= 1:
            t = tiles[step - 1]
            work.append([functools.partial(pv_block, t, j) for j in range(len(t.blocks))]
                        + [functools.partial(finish, t)])
        interleave(*work)

    @pl.when(jnp.logical_not(in_range))
    def _():
        for hh, i in itertools.product(range(n_heads), range(n_tiles)):
            t = Tile(i, hh)
            q2 = stacked_q(t)
            for blk in t.blocks:
                score_block(t, q2, blk)
            for j in range(len(t.blocks)):
                exp_block(t, j)
            for j in range(len(t.blocks)):
                pv_block(t, j)
            finish(t)


def _attn_call(lq1, lk1, lq2, lk2, gain, q, k, vt, sa, kmax, lam_init):
    bsz, seq, width = q.shape
    n_heads = width // V_DIM
    hw = ATTN_HEADS_PER_STEP * V_DIM
    head_spec = pl.BlockSpec((None, seq, hw), lambda b, h: (b, 0, h))
    vt_spec = pl.BlockSpec((None, hw, seq), lambda b, h: (b, h, 0))
    kmax_spec = pl.BlockSpec((None, 1, hw), lambda b, h: (b, 0, h))

    def full(shape):
        return pl.BlockSpec(shape, lambda b, h: (0,) * len(shape))

    return pl.pallas_call(
        functools.partial(_attn_kernel, lam_init=lam_init),
        grid=(bsz, n_heads // ATTN_HEADS_PER_STEP),
        in_specs=[full(lq1.shape), full(lk1.shape), full(lq2.shape), full(lk2.shape),
                  full(gain.shape), head_spec, head_spec, vt_spec, head_spec, kmax_spec],
        out_specs=head_spec,
        out_shape=jax.ShapeDtypeStruct((bsz, seq, width), jnp.bfloat16),
        compiler_params=pltpu.CompilerParams(
            dimension_semantics=("arbitrary", "arbitrary"),
            vmem_limit_bytes=V7X_VMEM_LIMIT_BYTES),
        name="diff_attention",
    )(lq1, lk1, lq2, lk2, gain, q, k, vt, sa, kmax)


def _out_kernel(oa_ref, c_ref, g_ref, x_ref, wa_f32_ref, wb_f32_ref, wo_f32_ref,
                lng_ref, lnb_ref, out_ref, wa_ref, wb_ref, wo_ref, *, alpha):
    tm, d = x_ref.shape
    cw = OUT_COLS
    f32 = jnp.float32

    @pl.when(jnp.logical_and(pl.program_id(0) == 0, pl.program_id(1) == 0))
    def _():
        for src, dst in ((wa_f32_ref, wa_ref), (wb_f32_ref, wb_ref), (wo_f32_ref, wo_ref)):
            for lo in range(0, d, cw):
                dst[:, lo:lo + cw] = src[:, lo:lo + cw].astype(dst.dtype)

    def merge_stage(rows, lo, merged):
        def dots():
            return [jnp.dot(oa_ref[rows, :], wa_ref[:, lo:lo + cw], preferred_element_type=f32),
                    jnp.dot(c_ref[rows, :], wb_ref[:, lo:lo + cw], preferred_element_type=f32)]

        def epilogue(ya, yb):
            merged.append((g_ref[rows, lo:lo + cw].astype(f32) * ya
                           + g_ref[rows, d + lo:d + lo + cw].astype(f32) * yb
                           ).astype(jnp.bfloat16))
        return dots, epilogue

    def out_stage(rows, lo, merged, resid):
        def dots():
            lhs = jnp.concatenate(merged, axis=1)
            return [jnp.dot(lhs, wo_ref[:, lo:lo + cw], preferred_element_type=f32)]

        def epilogue(out):
            resid.append(alpha * x_ref[rows, lo:lo + cw] + out)
            if lo + cw == d:
                mu = sum(jnp.sum(r, axis=-1, keepdims=True) for r in resid) / d
                dev = [r - mu for r in resid]
                var = sum(jnp.sum(v * v, axis=-1, keepdims=True) for v in dev) / d
                inv = lax.rsqrt(var + LN_EPS)
                for j, v in enumerate(dev):
                    cols = slice(j * cw, (j + 1) * cw)
                    out_ref[rows, cols] = v * inv * lng_ref[:, cols] + lnb_ref[:, cols]
        return dots, epilogue

    col_starts = range(0, d, cw)
    sub_tiles = []
    for r0 in range(0, tm, OUT_SUB_ROWS):
        rows = slice(r0, r0 + OUT_SUB_ROWS)
        merged, resid = [], []
        sub_tiles.append(([merge_stage(rows, lo, merged) for lo in col_starts],
                          [out_stage(rows, lo, merged, resid) for lo in col_starts]))
    assert len(sub_tiles) >= 2
    stages = list(sub_tiles[0][0])
    for prev, cur in zip(sub_tiles[:-1], sub_tiles[1:]):
        for m_stage, o_stage in zip(cur[0], prev[1]):
            stages += [m_stage, o_stage]
    stages += sub_tiles[-1][1]
    _run_pipelined(stages)


def _out_call(oa, c, g, x, wa, wb, wo, ln_g, ln_b, alpha, layer):
    bsz, seq, d = x.shape
    tm = OUT_ROWS
    row_spec = pl.BlockSpec((None, tm, d), lambda b, i: (b, i, 0))
    weight_spec = pl.BlockSpec((None, d, d), lambda b, i: (layer, 0, 0),
                               pipeline_mode=pl.Buffered(1))

    def full(shape):
        return pl.BlockSpec(shape, lambda b, i: (0,) * len(shape))

    return pl.pallas_call(
        functools.partial(_out_kernel, alpha=alpha),
        grid=(bsz, seq // tm),
        in_specs=[row_spec, row_spec,
                  pl.BlockSpec((None, tm, 2 * d), lambda b, i: (b, i, 0)),
                  row_spec, weight_spec, weight_spec, weight_spec,
                  full(ln_g.shape), full(ln_b.shape)],
        out_specs=row_spec,
        out_shape=jax.ShapeDtypeStruct((bsz, seq, d), jnp.float32),
        scratch_shapes=[pltpu.VMEM((d, d), jnp.bfloat16) for _ in range(3)],
        compiler_params=pltpu.CompilerParams(
            dimension_semantics=("arbitrary", "arbitrary"),
            vmem_limit_bytes=V7X_VMEM_LIMIT_BYTES),
        name="merge_out_layernorm",
    )(oa, c, g, x, wa, wb, wo, ln_g, ln_b)


def _rotary_tables(seq_len):
    half = HEAD_DIM // 2
    inv_freq = 1.0 / (ROPE_THETA ** (np.arange(half, dtype=np.float64) / half))
    ang = np.arange(seq_len, dtype=np.float64)[:, None] * inv_freq[None, :]
    cos = np.cos(ang)
    sin = np.sin(ang)
    reps = V7X_LANES // HEAD_DIM
    cos_t = np.tile(np.concatenate([cos, cos], -1), (1, reps))
    sin_t = np.tile(np.concatenate([-sin, sin], -1), (1, reps))
    return jnp.asarray(cos_t, jnp.float32), jnp.asarray(sin_t, jnp.float32)


def kernel(x, w_in, b_gate, lambda_q1, lambda_k1, lambda_q2, lambda_k2, subln_g,
           conv_w, conv_b, w_a_out, w_b_out, w_o, ln_g, ln_b):
    depth = w_in.shape[0]
    seq_len = x.shape[1]
    alpha = (2.0 * depth) ** 0.25
    cos_t, sin_t = _rotary_tables(seq_len)
    for l in range(depth):
        lam_init = 0.8 - 0.6 * math.exp(-0.3 * l)
        q, k, vt, sa, c, g, kmax = _proj_call(
            x, w_in, cos_t, sin_t,
            b_gate[l][None, :], conv_w[l], conv_b[l][None, :], l)
        oa = _attn_call(lambda_q1[l][None, :], lambda_k1[l][None, :],
                        lambda_q2[l][None, :], lambda_k2[l][None, :],
                        subln_g[l][None, :], q, k, vt, sa, kmax, lam_init)
        x = _out_call(oa, c, g, x, w_a_out, w_b_out, w_o,
                      ln_g[l][None, :], ln_b[l][None, :], alpha, l)
    return x
```

```python
import functools
import itertools
import math

import jax
import jax.numpy as jnp
import numpy as np
from jax import lax
from jax.experimental import pallas as pl
from jax.experimental.pallas import tpu as pltpu

CHUNK = 64
HEAD_DIM = 64
V_DIM = 2 * HEAD_DIM
ROPE_THETA = 10000.0
LN_EPS = 1e-5
RMS_EPS = 1e-5

V7X_LANES = 128
V7X_SUBLANES = 8
V7X_VMEM_LIMIT_BYTES = 56 * 1024 * 1024

PROJ_ROWS = 512
PROJ_COLS = 256
PROJ_STAGE_SLOTS = 8
PROJ_FETCH_GROUP = 2
ATTN_ROWS = 256
ATTN_KEYS = 2048
ATTN_HEADS_PER_STEP = 2
SOFTMAX_HEADROOM = 16.0
SOFTMAX_L_MIN = 2.0 ** -80
SOFTMAX_L_MAX = 2.0 ** 60
OUT_ROWS = 1024
OUT_SUB_ROWS = 256
OUT_COLS = 256

_NT_DIMS = (((1,), (1,)), ((), ()))


def _sigmoid(z):
    return 1.0 / (1.0 + jnp.exp(-z))


def _run_pipelined(stages, group=1, before_group=None):
    pending = None
    for g in range(0, len(stages), group):
        if before_group is not None:
            before_group(stages[g:g + group])
        for stage in stages[g:g + group]:
            dots, epilogue = stage[:2]
            acc = dots()
            if pending is not None:
                pending[0](*pending[1])
            pending = (epilogue, acc)
    pending[0](*pending[1])


def _proj_kernel(x_ref, w_hbm, cos_ref, sin_ref, bg_ref, cw_ref, cb_ref,
                 q_ref, k_ref, vt_ref, sa_ref, c_ref, g_ref, kmax_ref,
                 ubuf, w_ref, wvt_ref, stage, sem, *, d_model, layer):
    tm = x_ref.shape[0]
    d = d_model
    cw = PROJ_COLS

    xb = x_ref[...].astype(jnp.bfloat16)

    def proj(lo):
        return jnp.dot(xb, w_ref[:, lo:lo + cw], preferred_element_type=jnp.float32)

    cos = cos_ref[...]
    sin_signed = sin_ref[...]
    lane = lax.broadcasted_iota(jnp.int32, (tm, V7X_LANES), 1)
    first_half = (lane & (HEAD_DIM - 1)) < (HEAD_DIM // 2)

    def rotary(t):
        partner = jnp.where(first_half,
                            pltpu.roll(t, V7X_LANES - HEAD_DIM // 2, 1),
                            pltpu.roll(t, HEAD_DIM // 2, 1))
        return t * cos + partner * sin_signed

    scale = HEAD_DIM ** -0.5 * math.log2(math.e)

    def rotary_epilogue(out_ref, lo, mult, absmax_ref=None):
        def epilogue(t):
            for j in range(0, cw, V7X_LANES):
                sl = slice(lo + j, lo + j + V7X_LANES)
                r = rotary(t[:, j:j + V7X_LANES])
                if mult != 1.0:
                    r = r * mult
                out_ref[:, sl] = r.astype(out_ref.dtype)
                if absmax_ref is not None:
                    absmax_ref[:, sl] = jnp.maximum(
                        absmax_ref[:, sl], jnp.max(jnp.abs(r), axis=0, keepdims=True))
        return epilogue

    def vt_stage(lo):
        def dots():
            return [lax.dot_general(wvt_ref[lo:lo + cw, :], xb, _NT_DIMS,
                                    preferred_element_type=jnp.float32)]

        def epilogue(t):
            vt_ref[lo:lo + cw, :] = t.astype(vt_ref.dtype)
        return dots, epilogue, (2 * d + lo,)

    def silu_epilogue(lo):
        def epilogue(za):
            sa_ref[:, lo:lo + cw] = (za * _sigmoid(za)).astype(sa_ref.dtype)
        return epilogue

    def conv_in_epilogue(lo):
        def epilogue(h, cgate):
            ubuf[V7X_SUBLANES:V7X_SUBLANES + tm, lo:lo + cw] = cgate * h
        return epilogue

    def conv_out_epilogue(lo):
        sl = slice(lo, lo + cw)

        def epilogue(bgate, zb):
            u = ubuf[V7X_SUBLANES:V7X_SUBLANES + tm, sl]
            u1 = ubuf[V7X_SUBLANES - 1:V7X_SUBLANES - 1 + tm, sl]
            u2 = ubuf[V7X_SUBLANES - 2:V7X_SUBLANES - 2 + tm, sl]
            conv = (cw_ref[0:1, sl] * u2 + cw_ref[1:2, sl] * u1
                    + cw_ref[2:3, sl] * u + cb_ref[:, sl])
            c_ref[:, sl] = (bgate * conv * (zb * _sigmoid(zb))).astype(c_ref.dtype)
            ubuf[0:V7X_SUBLANES, sl] = ubuf[tm:tm + V7X_SUBLANES, sl]
        return epilogue

    def gate_epilogue(lo):
        def epilogue(gl):
            g_ref[:, lo:lo + cw] = _sigmoid(gl + bg_ref[:, lo:lo + cw]).astype(g_ref.dtype)
        return epilogue

    chunks = range(0, d, cw)
    def proj_stage(cols, epilogue):
        return (lambda: [proj(lo) for lo in cols]), epilogue, cols

    heavy = ([proj_stage((3 * d + lo,), silu_epilogue(lo)) for lo in chunks]
             + [proj_stage((8 * d + lo,), gate_epilogue(lo)) for lo in range(0, 2 * d, cw)]
             + [proj_stage((5 * d + lo, 7 * d + lo), conv_out_epilogue(lo)) for lo in chunks])
    light = [proj_stage((4 * d + lo, 6 * d + lo), conv_in_epilogue(lo)) for lo in chunks]
    for lo in chunks:
        light.append(proj_stage((lo,), rotary_epilogue(q_ref, lo, scale)))
        light.append(proj_stage((d + lo,), rotary_epilogue(k_ref, lo, 1.0, kmax_ref)))
    light += [vt_stage(lo) for lo in chunks]
    assert len(heavy) == len(light)
    stages = [stage for pair in zip(heavy, light) for stage in pair]

    @pl.when(pl.program_id(1) == 0)
    def _():
        ubuf[0:V7X_SUBLANES, :] = jnp.zeros((V7X_SUBLANES, d), jnp.float32)
        kmax_ref[...] = jnp.zeros_like(kmax_ref)

    order = [lo for st in stages for lo in st[2]]
    assert sorted(order) == list(range(0, w_ref.shape[1], cw))
    n_slots = stage.shape[0]

    def chunk_copy(n):
        return pltpu.make_async_copy(w_hbm.at[layer, :, pl.ds(order[n], cw)],
                                     stage.at[n % n_slots], sem.at[n % n_slots])

    def fetch_group(group, counter):
        for lo in [lo for st in group for lo in st[2]]:
            n = next(counter)
            assert order[n] == lo
            chunk_copy(n).wait()
            if 2 * d <= lo < 3 * d:
                wvt_ref[lo - 2 * d:lo - 2 * d + cw, :] = stage[n % n_slots].T.astype(wvt_ref.dtype)
            else:
                w_ref[:, lo:lo + cw] = stage[n % n_slots].astype(w_ref.dtype)
            if n + n_slots < len(order):
                chunk_copy(n + n_slots).start()

    first_step = jnp.logical_and(pl.program_id(0) == 0, pl.program_id(1) == 0)

    @pl.when(first_step)
    def _():
        for n in range(n_slots):
            chunk_copy(n).start()
        _run_pipelined(stages, group=PROJ_FETCH_GROUP,
                       before_group=functools.partial(fetch_group, counter=itertools.count()))

    @pl.when(jnp.logical_not(first_step))
    def _():
        _run_pipelined(stages)


def _proj_call(x, w_in, cos, sin_signed, b_gate, conv_w, conv_b, layer):
    bsz, seq, d = x.shape
    in_width = w_in.shape[2]
    assert in_width % PROJ_COLS == 0
    tm = PROJ_ROWS
    act = jax.ShapeDtypeStruct((bsz, seq, d), jnp.bfloat16)
    row_spec = pl.BlockSpec((None, tm, d), lambda b, i: (b, i, 0))
    tab_spec = pl.BlockSpec((tm, V7X_LANES), lambda b, i: (i, 0))

    def full(shape):
        return pl.BlockSpec(shape, lambda b, i: (0,) * len(shape))

    return pl.pallas_call(
        functools.partial(_proj_kernel, d_model=d, layer=layer),
        grid=(bsz, seq // tm),
        in_specs=[
            row_spec,
            pl.BlockSpec(memory_space=pl.ANY),
            tab_spec, tab_spec,
            full(b_gate.shape), full(conv_w.shape), full(conv_b.shape),
        ],
        out_specs=[row_spec, row_spec,
                   pl.BlockSpec((None, d, tm), lambda b, i: (b, 0, i)),
                   row_spec, row_spec,
                   pl.BlockSpec((None, tm, 2 * d), lambda b, i: (b, i, 0)),
                   pl.BlockSpec((None, 1, d), lambda b, i: (b, 0, 0))],
        out_shape=[act, act, jax.ShapeDtypeStruct((bsz, d, seq), jnp.bfloat16), act, act,
                   jax.ShapeDtypeStruct((bsz, seq, 2 * d), jnp.bfloat16),
                   jax.ShapeDtypeStruct((bsz, 1, d), jnp.float32)],
        scratch_shapes=[pltpu.VMEM((tm + V7X_SUBLANES, d), jnp.float32),
                        pltpu.VMEM((d, in_width), jnp.bfloat16),
                        pltpu.VMEM((d, d), jnp.bfloat16),
                        pltpu.VMEM((PROJ_STAGE_SLOTS, d, PROJ_COLS), jnp.float32),
                        pltpu.SemaphoreType.DMA((PROJ_STAGE_SLOTS,))],
        compiler_params=pltpu.CompilerParams(
            dimension_semantics=("arbitrary", "arbitrary"),
            vmem_limit_bytes=V7X_VMEM_LIMIT_BYTES),
        name="proj_rotary_conv_gates",
    )(x, w_in, cos, sin_signed, b_gate, conv_w, conv_b)


def _attn_kernel(lq1_ref, lk1_ref, lq2_ref, lk2_ref, gain_ref,
                 q_ref, k_ref, vt_ref, sa_ref, kmax_ref, o_ref, *, lam_init):
    seq = q_ref.shape[0]
    tq = ATTN_ROWS
    bk = ATTN_KEYS
    f32 = jnp.float32
    lam = (jnp.exp(jnp.sum(lq1_ref[...] * lk1_ref[...], axis=-1, keepdims=True))
           - jnp.exp(jnp.sum(lq2_ref[...] * lk2_ref[...], axis=-1, keepdims=True))
           + lam_init)

    lane = lax.broadcasted_iota(jnp.int32, (tq, V_DIM), 1)
    map1 = lane < HEAD_DIM
    key = lax.broadcasted_iota(jnp.int32, (tq, 2 * tq), 0)
    col = lax.broadcasted_iota(jnp.int32, (tq, 2 * tq), 1)
    qpos = jnp.where(col >= tq, col - tq, col)
    allowed = (key // CHUNK) <= (qpos // CHUNK)
    gain = gain_ref[...] * (1.0 - lam_init)

    def key_blocks(i):
        blocks = [(lo, min(bk, i * tq - lo), False) for lo in range(0, i * tq, bk)]
        return blocks + [(i * tq, tq, True)]

    def head_lanes(hh):
        return slice(hh * V_DIM, (hh + 1) * V_DIM)

    class Tile:
        def __init__(self, i, hh):
            self.i = i
            self.lanes = head_lanes(hh)
            self.blocks = key_blocks(i)
            self.s, self.e = [], []
            self.m = self.l = self.acc = self.ratio = None

    def stacked_q(t):
        q = q_ref[t.i * tq:(t.i + 1) * tq, t.lanes]
        zero = jnp.zeros_like(q)
        return jnp.concatenate([jnp.where(map1, q, zero), jnp.where(map1, zero, q)], axis=0)

    def score_block(t, q2, blk):
        lo, size, masked = blk
        s = lax.dot_general(k_ref[lo:lo + size, t.lanes], q2, _NT_DIMS,
                            preferred_element_type=f32)
        if masked:
            s = jnp.where(allowed, s, -jnp.inf)
        t.s.append(s)
        bm = jnp.max(s, axis=0, keepdims=True)
        t.m = bm if t.m is None else jnp.maximum(t.m, bm)

    def exp_block(t, j):
        e = jnp.exp2(t.s[j] - t.m)
        bl = jnp.sum(e, axis=0, keepdims=True)
        t.l = bl if t.l is None else t.l + bl
        t.e.append(e.astype(jnp.bfloat16))

    def pv_block(t, j):
        lo, size, _ = t.blocks[j]
        if t.ratio is None:
            t.ratio = (lam * t.l[:, :tq] / t.l[:, tq:]).astype(jnp.bfloat16)
        e = t.e[j]
        a = e[:, :tq] - t.ratio * e[:, tq:]
        pv = jnp.dot(vt_ref[t.lanes, lo:lo + size], a, preferred_element_type=f32)
        t.acc = pv if t.acc is None else t.acc + pv

    def finish(t):
        rows = slice(t.i * tq, (t.i + 1) * tq)
        o = t.acc / t.l[:, :tq]
        ms = jnp.mean(o * o, axis=0, keepdims=True)
        y = (o * lax.rsqrt(ms + RMS_EPS)).T * gain
        o_ref[rows, t.lanes] = (y * sa_ref[rows, t.lanes].astype(f32)).astype(o_ref.dtype)

    def interleave(*work):
        for group in itertools.zip_longest(*work):
            for fn in group:
                if fn is not None:
                    fn()

    n_tiles = seq // tq
    n_heads = q_ref.shape[1] // V_DIM

    kabs_max = kmax_ref[...].astype(jnp.bfloat16).astype(f32)
    lane_q = lax.broadcasted_iota(jnp.int32, (2 * tq, V7X_LANES), 1)

    @functools.cache
    def ones_lane0(size):
        lane_k = lax.broadcasted_iota(jnp.int32, (size, V7X_LANES), 1)
        return jnp.where(lane_k == 0, 1.0, 0.0).astype(jnp.bfloat16)

    def shifted_q(t):
        q2 = stacked_q(t)
        bound = jnp.sum(jnp.abs(q2.astype(f32)) * kabs_max[:, t.lanes], axis=-1, keepdims=True)
        neg_shift = jnp.where(lane_q == 0, SOFTMAX_HEADROOM - bound, 0.0)
        return jnp.concatenate([q2, neg_shift.astype(jnp.bfloat16)], axis=1)

    def shifted_exp_block(t, q2a, blk):
        lo, size, masked = blk
        k_aug = jnp.concatenate([k_ref[lo:lo + size, t.lanes], ones_lane0(size)], axis=1)
        s = lax.dot_general(k_aug, q2a, _NT_DIMS, preferred_element_type=f32)
        if masked:
            s = jnp.where(allowed, s, -jnp.inf)
        e = jnp.exp2(s)
        bl = jnp.sum(e, axis=0, keepdims=True)
        t.l = bl if t.l is None else t.l + bl
        t.e.append(e.astype(jnp.bfloat16))

    order = list(range(1, n_tiles, 2)) + list(reversed(range(0, n_tiles, 2)))
    tiles = [Tile(i, hh) for hh in range(n_heads) for i in order]
    for step in range(len(tiles) + 1):
        work = []
        if step < len(tiles):
            t = tiles[step]
            q2a = shifted_q(t)
            work.append([functools.partial(shifted_exp_block, t, q2a, blk) for blk in t.blocks])
        else:
            l_min = functools.reduce(jnp.minimum, [t.l for t in tiles])
            l_max = functools.reduce(jnp.maximum, [t.l for t in tiles])
            in_range = jnp.logical_and(jnp.min(l_min) >= SOFTMAX_L_MIN,
                                       jnp.max(l_max) <= SOFTMAX_L_MAX)
        if step >= 1:
            t = tiles[step - 1]
            work.append([functools.partial(pv_block, t, j) for j in range(len(t.blocks))]
                        + [functools.partial(finish, t)])
        interleave(*work)

    @pl.when(jnp.logical_not(in_range))
    def _():
        for hh, i in itertools.product(range(n_heads), range(n_tiles)):
            t = Tile(i, hh)
            q2 = stacked_q(t)
            for blk in t.blocks:
                score_block(t, q2, blk)
            for j in range(len(t.blocks)):
                exp_block(t, j)
            for j in range(len(t.blocks)):
                pv_block(t, j)
            finish(t)


def _attn_call(lq1, lk1, lq2, lk2, gain, q, k, vt, sa, kmax, lam_init):
    bsz, seq, width = q.shape
    n_heads = width // V_DIM
    hw = ATTN_HEADS_PER_STEP * V_DIM
    head_spec = pl.BlockSpec((None, seq, hw), lambda b, h: (b, 0, h))
    vt_spec = pl.BlockSpec((None, hw, seq), lambda b, h: (b, h, 0))
    kmax_spec = pl.BlockSpec((None, 1, hw), lambda b, h: (b, 0, h))

    def full(shape):
        return pl.BlockSpec(shape, lambda b, h: (0,) * len(shape))

    return pl.pallas_call(
        functools.partial(_attn_kernel, lam_init=lam_init),
        grid=(bsz, n_heads // ATTN_HEADS_PER_STEP),
        in_specs=[full(lq1.shape), full(lk1.shape), full(lq2.shape), full(lk2.shape),
                  full(gain.shape), head_spec, head_spec, vt_spec, head_spec, kmax_spec],
        out_specs=head_spec,
        out_shape=jax.ShapeDtypeStruct((bsz, seq, width), jnp.bfloat16),
        compiler_params=pltpu.CompilerParams(
            dimension_semantics=("arbitrary", "arbitrary"),
            vmem_limit_bytes=V7X_VMEM_LIMIT_BYTES),
        name="diff_attention",
    )(lq1, lk1, lq2, lk2, gain, q, k, vt, sa, kmax)


def _out_kernel(oa_ref, c_ref, g_ref, x_ref, wa_f32_ref, wb_f32_ref, wo_f32_ref,
                lng_ref, lnb_ref, out_ref, wa_ref, wb_ref, wo_ref, *, alpha):
    tm, d = x_ref.shape
    cw = OUT_COLS
    f32 = jnp.float32

    @pl.when(jnp.logical_and(pl.program_id(0) == 0, pl.program_id(1) == 0))
    def _():
        for src, dst in ((wa_f32_ref, wa_ref), (wb_f32_ref, wb_ref), (wo_f32_ref, wo_ref)):
            for lo in range(0, d, cw):
                dst[:, lo:lo + cw] = src[:, lo:lo + cw].astype(dst.dtype)

    def merge_stage(rows, lo, merged):
        def dots():
            return [jnp.dot(oa_ref[rows, :], wa_ref[:, lo:lo + cw], preferred_element_type=f32),
                    jnp.dot(c_ref[rows, :], wb_ref[:, lo:lo + cw], preferred_element_type=f32)]

        def epilogue(ya, yb):
            merged.append((g_ref[rows, lo:lo + cw].astype(f32) * ya
                           + g_ref[rows, d + lo:d + lo + cw].astype(f32) * yb
                           ).astype(jnp.bfloat16))
        return dots, epilogue

    def out_stage(rows, lo, merged, resid):
        def dots():
            lhs = jnp.concatenate(merged, axis=1)
            return [jnp.dot(lhs, wo_ref[:, lo:lo + cw], preferred_element_type=f32)]

        def epilogue(out):
            resid.append(alpha * x_ref[rows, lo:lo + cw] + out)
            if lo + cw == d:
                mu = sum(jnp.sum(r, axis=-1, keepdims=True) for r in resid) / d
                dev = [r - mu for r in resid]
                var = sum(jnp.sum(v * v, axis=-1, keepdims=True) for v in dev) / d
                inv = lax.rsqrt(var + LN_EPS)
                for j, v in enumerate(dev):
                    cols = slice(j * cw, (j + 1) * cw)
                    out_ref[rows, cols] = v * inv * lng_ref[:, cols] + lnb_ref[:, cols]
        return dots, epilogue

    col_starts = range(0, d, cw)
    sub_tiles = []
    for r0 in range(0, tm, OUT_SUB_ROWS):
        rows = slice(r0, r0 + OUT_SUB_ROWS)
        merged, resid = [], []
        sub_tiles.append(([merge_stage(rows, lo, merged) for lo in col_starts],
                          [out_stage(rows, lo, merged, resid) for lo in col_starts]))
    assert len(sub_tiles) >= 2
    stages = list(sub_tiles[0][0])
    for prev, cur in zip(sub_tiles[:-1], sub_tiles[1:]):
        for m_stage, o_stage in zip(cur[0], prev[1]):
            stages += [m_stage, o_stage]
    stages += sub_tiles[-1][1]
    _run_pipelined(stages)


def _out_call(oa, c, g, x, wa, wb, wo, ln_g, ln_b, alpha, layer):
    bsz, seq, d = x.shape
    tm = OUT_ROWS
    row_spec = pl.BlockSpec((None, tm, d), lambda b, i: (b, i, 0))
    weight_spec = pl.BlockSpec((None, d, d), lambda b, i: (layer, 0, 0),
                               pipeline_mode=pl.Buffered(1))

    def full(shape):
        return pl.BlockSpec(shape, lambda b, i: (0,) * len(shape))

    return pl.pallas_call(
        functools.partial(_out_kernel, alpha=alpha),
        grid=(bsz, seq // tm),
        in_specs=[row_spec, row_spec,
                  pl.BlockSpec((None, tm, 2 * d), lambda b, i: (b, i, 0)),
                  row_spec, weight_spec, weight_spec, weight_spec,
                  full(ln_g.shape), full(ln_b.shape)],
        out_specs=row_spec,
        out_shape=jax.ShapeDtypeStruct((bsz, seq, d), jnp.float32),
        scratch_shapes=[pltpu.VMEM((d, d), jnp.bfloat16) for _ in range(3)],
        compiler_params=pltpu.CompilerParams(
            dimension_semantics=("arbitrary", "arbitrary"),
            vmem_limit_bytes=V7X_VMEM_LIMIT_BYTES),
        name="merge_out_layernorm",
    )(oa, c, g, x, wa, wb, wo, ln_g, ln_b)


def _rotary_tables(seq_len):
    half = HEAD_DIM // 2
    inv_freq = 1.0 / (ROPE_THETA ** (np.arange(half, dtype=np.float64) / half))
    ang = np.arange(seq_len, dtype=np.float64)[:, None] * inv_freq[None, :]
    cos = np.cos(ang)
    sin = np.sin(ang)
    reps = V7X_LANES // HEAD_DIM
    cos_t = np.tile(np.concatenate([cos, cos], -1), (1, reps))
    sin_t = np.tile(np.concatenate([-sin, sin], -1), (1, reps))
    return jnp.asarray(cos_t, jnp.float32), jnp.asarray(sin_t, jnp.float32)


def kernel(x, w_in, b_gate, lambda_q1, lambda_k1, lambda_q2, lambda_k2, subln_g,
           conv_w, conv_b, w_a_out, w_b_out, w_o, ln_g, ln_b):
    depth = w_in.shape[0]
    seq_len = x.shape[1]
    alpha = (2.0 * depth) ** 0.25
    cos_t, sin_t = _rotary_tables(seq_len)
    for l in range(depth):
        lam_init = 0.8 - 0.6 * math.exp(-0.3 * l)
        q, k, vt, sa, c, g, kmax = _proj_call(
            x, w_in, cos_t, sin_t,
            b_gate[l][None, :], conv_w[l], conv_b[l][None, :], l)
        oa = _attn_call(lambda_q1[l][None, :], lambda_k1[l][None, :],
                        lambda_q2[l][None, :], lambda_k2[l][None, :],
                        subln_g[l][None, :], q, k, vt, sa, kmax, lam_init)
        x = _out_call(oa, c, g, x, w_a_out, w_b_out, w_o,
                      ln_g[l][None, :], ln_b[l][None, :], alpha, l)
    return x
```

```python
import functools
import itertools
import math

import jax
import jax.numpy as jnp
import numpy as np
from jax import lax
from jax.experimental import pallas as pl
from jax.experimental.pallas import tpu as pltpu

CHUNK = 64
HEAD_DIM = 64
V_DIM = 2 * HEAD_DIM
ROPE_THETA = 10000.0
LN_EPS = 1e-5
RMS_EPS = 1e-5

V7X_LANES = 128
V7X_SUBLANES = 8
V7X_VMEM_LIMIT_BYTES = 56 * 1024 * 1024

PROJ_ROWS = 512
PROJ_COLS = 256
PROJ_CAST_COLS = 512
ATTN_ROWS = 256
ATTN_KEYS = 2048
ATTN_HEADS_PER_STEP = 2
SOFTMAX_HEADROOM = 16.0
SOFTMAX_L_MIN = 2.0 ** -80
SOFTMAX_L_MAX = 2.0 ** 60
OUT_ROWS = 1024
OUT_SUB_ROWS = 256
OUT_COLS = 256

_NT_DIMS = (((1,), (1,)), ((), ()))


def _sigmoid(z):
    return 1.0 / (1.0 + jnp.exp(-z))


def _run_pipelined(stages):
    pending = None
    for dots, epilogue in stages:
        acc = dots()
        if pending is not None:
            pending[0](*pending[1])
        pending = (epilogue, acc)
    pending[0](*pending[1])


def _proj_kernel(x_ref, w_hbm, cos_ref, sin_ref, bg_ref, cw_ref, cb_ref,
                 wa_f32_ref, wb_f32_ref, wo_f32_ref,
                 q_ref, k_ref, vt_ref, sa_ref, c_ref, g_ref, kmax_ref,
                 wa_bf_ref, wb_bf_ref, wo_bf_ref,
                 ubuf, w_ref, wvt_ref, stage, sem, *, d_model, layer):
    tm = x_ref.shape[0]
    d = d_model
    cw = PROJ_COLS

    for src, dst in ((wa_f32_ref, wa_bf_ref), (wb_f32_ref, wb_bf_ref), (wo_f32_ref, wo_bf_ref)):
        dst[...] = src[...].astype(dst.dtype)

    @pl.when(jnp.logical_and(pl.program_id(0) == 0, pl.program_id(1) == 0))
    def _():
        cc = stage.shape[2]
        n_chunks = w_ref.shape[1] // cc

        def chunk_copy(c):
            return pltpu.make_async_copy(w_hbm.at[layer, :, pl.ds(c * cc, cc)],
                                         stage.at[c % 2], sem.at[c % 2])

        chunk_copy(0).start()
        for c in range(n_chunks):
            if c + 1 < n_chunks:
                chunk_copy(c + 1).start()
            chunk_copy(c).wait()
            lo = c * cc
            if 2 * d <= lo < 3 * d:
                wvt_ref[lo - 2 * d:lo - 2 * d + cc, :] = stage[c % 2].T.astype(wvt_ref.dtype)
            else:
                w_ref[:, lo:lo + cc] = stage[c % 2].astype(w_ref.dtype)

    xb = x_ref[...].astype(jnp.bfloat16)

    def proj(lo):
        return jnp.dot(xb, w_ref[:, lo:lo + cw], preferred_element_type=jnp.float32)

    cos = cos_ref[...]
    sin_signed = sin_ref[...]
    lane = lax.broadcasted_iota(jnp.int32, (tm, V7X_LANES), 1)
    first_half = (lane & (HEAD_DIM - 1)) < (HEAD_DIM // 2)

    def rotary(t):
        partner = jnp.where(first_half,
                            pltpu.roll(t, V7X_LANES - HEAD_DIM // 2, 1),
                            pltpu.roll(t, HEAD_DIM // 2, 1))
        return t * cos + partner * sin_signed

    scale = HEAD_DIM ** -0.5 * math.log2(math.e)

    def rotary_epilogue(out_ref, lo, mult, absmax_ref=None):
        def epilogue(t):
            for j in range(0, cw, V7X_LANES):
                sl = slice(lo + j, lo + j + V7X_LANES)
                r = rotary(t[:, j:j + V7X_LANES])
                if mult != 1.0:
                    r = r * mult
                out_ref[:, sl] = r.astype(out_ref.dtype)
                if absmax_ref is not None:
                    absmax_ref[:, sl] = jnp.maximum(
                        absmax_ref[:, sl], jnp.max(jnp.abs(r), axis=0, keepdims=True))
        return epilogue

    def vt_stage(lo):
        def dots():
            return [lax.dot_general(wvt_ref[lo:lo + cw, :], xb, _NT_DIMS,
                                    preferred_element_type=jnp.float32)]

        def epilogue(t):
            vt_ref[lo:lo + cw, :] = t.astype(vt_ref.dtype)
        return dots, epilogue

    def silu_epilogue(lo):
        def epilogue(za):
            sa_ref[:, lo:lo + cw] = (za * _sigmoid(za)).astype(sa_ref.dtype)
        return epilogue

    def conv_in_epilogue(lo):
        def epilogue(h, cgate):
            ubuf[V7X_SUBLANES:V7X_SUBLANES + tm, lo:lo + cw] = cgate * h
        return epilogue

    def conv_out_epilogue(lo):
        sl = slice(lo, lo + cw)

        def epilogue(bgate, zb):
            u = ubuf[V7X_SUBLANES:V7X_SUBLANES + tm, sl]
            u1 = ubuf[V7X_SUBLANES - 1:V7X_SUBLANES - 1 + tm, sl]
            u2 = ubuf[V7X_SUBLANES - 2:V7X_SUBLANES - 2 + tm, sl]
            conv = (cw_ref[0:1, sl] * u2 + cw_ref[1:2, sl] * u1
                    + cw_ref[2:3, sl] * u + cb_ref[:, sl])
            c_ref[:, sl] = (bgate * conv * (zb * _sigmoid(zb))).astype(c_ref.dtype)
            ubuf[0:V7X_SUBLANES, sl] = ubuf[tm:tm + V7X_SUBLANES, sl]
        return epilogue

    def gate_epilogue(lo):
        def epilogue(gl):
            g_ref[:, lo:lo + cw] = _sigmoid(gl + bg_ref[:, lo:lo + cw]).astype(g_ref.dtype)
        return epilogue

    chunks = range(0, d, cw)
    def proj_stage(cols, epilogue):
        return (lambda: [proj(lo) for lo in cols]), epilogue

    heavy = ([proj_stage((3 * d + lo,), silu_epilogue(lo)) for lo in chunks]
             + [proj_stage((8 * d + lo,), gate_epilogue(lo)) for lo in range(0, 2 * d, cw)]
             + [proj_stage((5 * d + lo, 7 * d + lo), conv_out_epilogue(lo)) for lo in chunks])
    light = [proj_stage((4 * d + lo, 6 * d + lo), conv_in_epilogue(lo)) for lo in chunks]
    for lo in chunks:
        light.append(proj_stage((lo,), rotary_epilogue(q_ref, lo, scale)))
        light.append(proj_stage((d + lo,), rotary_epilogue(k_ref, lo, 1.0, kmax_ref)))
    light += [vt_stage(lo) for lo in chunks]
    assert len(heavy) == len(light)
    stages = [stage for pair in zip(heavy, light) for stage in pair]

    @pl.when(pl.program_id(1) == 0)
    def _():
        ubuf[0:V7X_SUBLANES, :] = jnp.zeros((V7X_SUBLANES, d), jnp.float32)
        kmax_ref[...] = jnp.zeros_like(kmax_ref)

    _run_pipelined(stages)


def _proj_call(x, w_in, cos, sin_signed, b_gate, conv_w, conv_b, out_weights, layer):
    bsz, seq, d = x.shape
    in_width = w_in.shape[2]
    assert in_width % PROJ_CAST_COLS == 0
    tm = PROJ_ROWS
    n_i = seq // tm
    slab = d // (bsz * n_i)
    assert slab * bsz * n_i == d and slab % (2 * V7X_SUBLANES) == 0
    slab_in = pl.BlockSpec((None, slab, d), lambda b, i: (layer, b * n_i + i, 0))
    slab_out = pl.BlockSpec((slab, d), lambda b, i: (b * n_i + i, 0))
    w_bf = jax.ShapeDtypeStruct((d, d), jnp.bfloat16)
    act = jax.ShapeDtypeStruct((bsz, seq, d), jnp.bfloat16)
    row_spec = pl.BlockSpec((None, tm, d), lambda b, i: (b, i, 0))
    tab_spec = pl.BlockSpec((tm, V7X_LANES), lambda b, i: (i, 0))

    def full(shape):
        return pl.BlockSpec(shape, lambda b, i: (0,) * len(shape))

    return pl.pallas_call(
        functools.partial(_proj_kernel, d_model=d, layer=layer),
        grid=(bsz, seq // tm),
        in_specs=[
            row_spec,
            pl.BlockSpec(memory_space=pl.ANY),
            tab_spec, tab_spec,
            full(b_gate.shape), full(conv_w.shape), full(conv_b.shape),
            slab_in, slab_in, slab_in,
        ],
        out_specs=[row_spec, row_spec,
                   pl.BlockSpec((None, d, tm), lambda b, i: (b, 0, i)),
                   row_spec, row_spec,
                   pl.BlockSpec((None, tm, 2 * d), lambda b, i: (b, i, 0)),
                   pl.BlockSpec((None, 1, d), lambda b, i: (b, 0, 0)),
                   slab_out, slab_out, slab_out],
        out_shape=[act, act, jax.ShapeDtypeStruct((bsz, d, seq), jnp.bfloat16), act, act,
                   jax.ShapeDtypeStruct((bsz, seq, 2 * d), jnp.bfloat16),
                   jax.ShapeDtypeStruct((bsz, 1, d), jnp.float32), w_bf, w_bf, w_bf],
        scratch_shapes=[pltpu.VMEM((tm + V7X_SUBLANES, d), jnp.float32),
                        pltpu.VMEM((d, in_width), jnp.bfloat16),
                        pltpu.VMEM((d, d), jnp.bfloat16),
                        pltpu.VMEM((2, d, PROJ_CAST_COLS), jnp.float32),
                        pltpu.SemaphoreType.DMA((2,))],
        compiler_params=pltpu.CompilerParams(
            dimension_semantics=("arbitrary", "arbitrary"),
            vmem_limit_bytes=V7X_VMEM_LIMIT_BYTES),
        name="proj_rotary_conv_gates",
    )(x, w_in, cos, sin_signed, b_gate, conv_w, conv_b, *out_weights)


def _attn_kernel(lq1_ref, lk1_ref, lq2_ref, lk2_ref, gain_ref,
                 q_ref, k_ref, vt_ref, sa_ref, kmax_ref, o_ref, *, lam_init):
    seq = q_ref.shape[0]
    tq = ATTN_ROWS
    bk = ATTN_KEYS
    f32 = jnp.float32
    lam = (jnp.exp(jnp.sum(lq1_ref[...] * lk1_ref[...], axis=-1, keepdims=True))
           - jnp.exp(jnp.sum(lq2_ref[...] * lk2_ref[...], axis=-1, keepdims=True))
           + lam_init)

    lane = lax.broadcasted_iota(jnp.int32, (tq, V_DIM), 1)
    map1 = lane < HEAD_DIM
    key = lax.broadcasted_iota(jnp.int32, (tq, 2 * tq), 0)
    col = lax.broadcasted_iota(jnp.int32, (tq, 2 * tq), 1)
    qpos = jnp.where(col >= tq, col - tq, col)
    allowed = (key // CHUNK) <= (qpos // CHUNK)
    gain = gain_ref[...] * (1.0 - lam_init)

    def key_blocks(i):
        blocks = [(lo, min(bk, i * tq - lo), False) for lo in range(0, i * tq, bk)]
        return blocks + [(i * tq, tq, True)]

    def head_lanes(hh):
        return slice(hh * V_DIM, (hh + 1) * V_DIM)

    class Tile:
        def __init__(self, i, hh):
            self.i = i
            self.lanes = head_lanes(hh)
            self.blocks = key_blocks(i)
            self.s, self.e = [], []
            self.m = self.l = self.acc = self.ratio = None

    def stacked_q(t):
        q = q_ref[t.i * tq:(t.i + 1) * tq, t.lanes]
        zero = jnp.zeros_like(q)
        return jnp.concatenate([jnp.where(map1, q, zero), jnp.where(map1, zero, q)], axis=0)

    def score_block(t, q2, blk):
        lo, size, masked = blk
        s = lax.dot_general(k_ref[lo:lo + size, t.lanes], q2, _NT_DIMS,
                            preferred_element_type=f32)
        if masked:
            s = jnp.where(allowed, s, -jnp.inf)
        t.s.append(s)
        bm = jnp.max(s, axis=0, keepdims=True)
        t.m = bm if t.m is None else jnp.maximum(t.m, bm)

    def exp_block(t, j):
        e = jnp.exp2(t.s[j] - t.m)
        bl = jnp.sum(e, axis=0, keepdims=True)
        t.l = bl if t.l is None else t.l + bl
        t.e.append(e.astype(jnp.bfloat16))

    def pv_block(t, j):
        lo, size, _ = t.blocks[j]
        if t.ratio is None:
            t.ratio = (lam * t.l[:, :tq] / t.l[:, tq:]).astype(jnp.bfloat16)
        e = t.e[j]
        a = e[:, :tq] - t.ratio * e[:, tq:]
        pv = jnp.dot(vt_ref[t.lanes, lo:lo + size], a, preferred_element_type=f32)
        t.acc = pv if t.acc is None else t.acc + pv

    def finish(t):
        rows = slice(t.i * tq, (t.i + 1) * tq)
        o = t.acc / t.l[:, :tq]
        ms = jnp.mean(o * o, axis=0, keepdims=True)
        y = (o * lax.rsqrt(ms + RMS_EPS)).T * gain
        o_ref[rows, t.lanes] = (y * sa_ref[rows, t.lanes].astype(f32)).astype(o_ref.dtype)

    def interleave(*work):
        for group in itertools.zip_longest(*work):
            for fn in group:
                if fn is not None:
                    fn()

    n_tiles = seq // tq
    n_heads = q_ref.shape[1] // V_DIM

    kabs_max = kmax_ref[...].astype(jnp.bfloat16).astype(f32)
    lane_q = lax.broadcasted_iota(jnp.int32, (2 * tq, V7X_LANES), 1)

    @functools.cache
    def ones_lane0(size):
        lane_k = lax.broadcasted_iota(jnp.int32, (size, V7X_LANES), 1)
        return jnp.where(lane_k == 0, 1.0, 0.0).astype(jnp.bfloat16)

    def shifted_q(t):
        q2 = stacked_q(t)
        bound = jnp.sum(jnp.abs(q2.astype(f32)) * kabs_max[:, t.lanes], axis=-1, keepdims=True)
        neg_shift = jnp.where(lane_q == 0, SOFTMAX_HEADROOM - bound, 0.0)
        return jnp.concatenate([q2, neg_shift.astype(jnp.bfloat16)], axis=1)

    def shifted_exp_block(t, q2a, blk):
        lo, size, masked = blk
        k_aug = jnp.concatenate([k_ref[lo:lo + size, t.lanes], ones_lane0(size)], axis=1)
        s = lax.dot_general(k_aug, q2a, _NT_DIMS, preferred_element_type=f32)
        if masked:
            s = jnp.where(allowed, s, -jnp.inf)
        e = jnp.exp2(s)
        bl = jnp.sum(e, axis=0, keepdims=True)
        t.l = bl if t.l is None else t.l + bl
        t.e.append(e.astype(jnp.bfloat16))

    order = list(range(1, n_tiles, 2)) + list(reversed(range(0, n_tiles, 2)))
    tiles = [Tile(i, hh) for hh in range(n_heads) for i in order]
    for step in range(len(tiles) + 1):
        work = []
        if step < len(tiles):
            t = tiles[step]
            q2a = shifted_q(t)
            work.append([functools.partial(shifted_exp_block, t, q2a, blk) for blk in t.blocks])
        else:
            l_min = functools.reduce(jnp.minimum, [t.l for t in tiles])
            l_max = functools.reduce(jnp.maximum, [t.l for t in tiles])
            in_range = jnp.logical_and(jnp.min(l_min) >= SOFTMAX_L_MIN,
                                       jnp.max(l_max) <= SOFTMAX_L_MAX)
        if step >= 1:
            t = tiles[step - 1]
            work.append([functools.partial(pv_block, t, j) for j in range(len(t.blocks))]
                        + [functools.partial(finish, t)])
        interleave(*work)

    @pl.when(jnp.logical_not(in_range))
    def _():
        for hh, i in itertools.product(range(n_heads), range(n_tiles)):
            t = Tile(i, hh)
            q2 = stacked_q(t)
            for blk in t.blocks:
                score_block(t, q2, blk)
            for j in range(len(t.blocks)):
                exp_block(t, j)
            for j in range(len(t.blocks)):
                pv_block(t, j)
            finish(t)


def _attn_call(lq1, lk1, lq2, lk2, gain, q, k, vt, sa, kmax, lam_init):
    bsz, seq, width = q.shape
    n_heads = width // V_DIM
    hw = ATTN_HEADS_PER_STEP * V_DIM
    head_spec = pl.BlockSpec((None, seq, hw), lambda b, h: (b, 0, h))
    vt_spec = pl.BlockSpec((None, hw, seq), lambda b, h: (b, h, 0))
    kmax_spec = pl.BlockSpec((None, 1, hw), lambda b, h: (b, 0, h))

    def full(shape):
        return pl.BlockSpec(shape, lambda b, h: (0,) * len(shape))

    return pl.pallas_call(
        functools.partial(_attn_kernel, lam_init=lam_init),
        grid=(bsz, n_heads // ATTN_HEADS_PER_STEP),
        in_specs=[full(lq1.shape), full(lk1.shape), full(lq2.shape), full(lk2.shape),
                  full(gain.shape), head_spec, head_spec, vt_spec, head_spec, kmax_spec],
        out_specs=head_spec,
        out_shape=jax.ShapeDtypeStruct((bsz, seq, width), jnp.bfloat16),
        compiler_params=pltpu.CompilerParams(
            dimension_semantics=("arbitrary", "arbitrary"),
            vmem_limit_bytes=V7X_VMEM_LIMIT_BYTES),
        name="diff_attention",
    )(lq1, lk1, lq2, lk2, gain, q, k, vt, sa, kmax)


def _out_kernel(oa_ref, c_ref, g_ref, x_ref, wa_ref, wb_ref, wo_ref,
                lng_ref, lnb_ref, out_ref, *, alpha):
    tm, d = x_ref.shape
    cw = OUT_COLS
    f32 = jnp.float32

    def merge_stage(rows, lo, merged):
        def dots():
            return [jnp.dot(oa_ref[rows, :], wa_ref[:, lo:lo + cw], preferred_element_type=f32),
                    jnp.dot(c_ref[rows, :], wb_ref[:, lo:lo + cw], preferred_element_type=f32)]

        def epilogue(ya, yb):
            merged.append((g_ref[rows, lo:lo + cw].astype(f32) * ya
                           + g_ref[rows, d + lo:d + lo + cw].astype(f32) * yb
                           ).astype(jnp.bfloat16))
        return dots, epilogue

    def out_stage(rows, lo, merged, resid):
        def dots():
            lhs = jnp.concatenate(merged, axis=1)
            return [jnp.dot(lhs, wo_ref[:, lo:lo + cw], preferred_element_type=f32)]

        def epilogue(out):
            resid.append(alpha * x_ref[rows, lo:lo + cw] + out)
            if lo + cw == d:
                mu = sum(jnp.sum(r, axis=-1, keepdims=True) for r in resid) / d
                dev = [r - mu for r in resid]
                var = sum(jnp.sum(v * v, axis=-1, keepdims=True) for v in dev) / d
                inv = lax.rsqrt(var + LN_EPS)
                for j, v in enumerate(dev):
                    cols = slice(j * cw, (j + 1) * cw)
                    out_ref[rows, cols] = v * inv * lng_ref[:, cols] + lnb_ref[:, cols]
        return dots, epilogue

    col_starts = range(0, d, cw)
    sub_tiles = []
    for r0 in range(0, tm, OUT_SUB_ROWS):
        rows = slice(r0, r0 + OUT_SUB_ROWS)
        merged, resid = [], []
        sub_tiles.append(([merge_stage(rows, lo, merged) for lo in col_starts],
                          [out_stage(rows, lo, merged, resid) for lo in col_starts]))
    assert len(sub_tiles) >= 2
    stages = list(sub_tiles[0][0])
    for prev, cur in zip(sub_tiles[:-1], sub_tiles[1:]):
        for m_stage, o_stage in zip(cur[0], prev[1]):
            stages += [m_stage, o_stage]
    stages += sub_tiles[-1][1]
    _run_pipelined(stages)


def _out_call(oa, c, g, x, wa, wb, wo, ln_g, ln_b, alpha):
    bsz, seq, d = x.shape
    tm = OUT_ROWS
    row_spec = pl.BlockSpec((None, tm, d), lambda b, i: (b, i, 0))
    weight_spec = pl.BlockSpec((d, d), lambda b, i: (0, 0), pipeline_mode=pl.Buffered(1))

    def full(shape):
        return pl.BlockSpec(shape, lambda b, i: (0,) * len(shape))

    return pl.pallas_call(
        functools.partial(_out_kernel, alpha=alpha),
        grid=(bsz, seq // tm),
        in_specs=[row_spec, row_spec,
                  pl.BlockSpec((None, tm, 2 * d), lambda b, i: (b, i, 0)),
                  row_spec, weight_spec, weight_spec, weight_spec,
                  full(ln_g.shape), full(ln_b.shape)],
        out_specs=row_spec,
        out_shape=jax.ShapeDtypeStruct((bsz, seq, d), jnp.float32),
        compiler_params=pltpu.CompilerParams(
            dimension_semantics=("arbitrary", "arbitrary"),
            vmem_limit_bytes=V7X_VMEM_LIMIT_BYTES),
        name="merge_out_layernorm",
    )(oa, c, g, x, wa, wb, wo, ln_g, ln_b)


def _rotary_tables(seq_len):
    half = HEAD_DIM // 2
    inv_freq = 1.0 / (ROPE_THETA ** (np.arange(half, dtype=np.float64) / half))
    ang = np.arange(seq_len, dtype=np.float64)[:, None] * inv_freq[None, :]
    cos = np.cos(ang)
    sin = np.sin(ang)
    reps = V7X_LANES // HEAD_DIM
    cos_t = np.tile(np.concatenate([cos, cos], -1), (1, reps))
    sin_t = np.tile(np.concatenate([-sin, sin], -1), (1, reps))
    return jnp.asarray(cos_t, jnp.float32), jnp.asarray(sin_t, jnp.float32)


def kernel(x, w_in, b_gate, lambda_q1, lambda_k1, lambda_q2, lambda_k2, subln_g,
           conv_w, conv_b, w_a_out, w_b_out, w_o, ln_g, ln_b):
    depth = w_in.shape[0]
    seq_len = x.shape[1]
    alpha = (2.0 * depth) ** 0.25
    cos_t, sin_t = _rotary_tables(seq_len)
    for l in range(depth):
        lam_init = 0.8 - 0.6 * math.exp(-0.3 * l)
        q, k, vt, sa, c, g, kmax, wa_bf, wb_bf, wo_bf = _proj_call(
            x, w_in, cos_t, sin_t,
            b_gate[l][None, :], conv_w[l], conv_b[l][None, :], (w_a_out, w_b_out, w_o), l)
        oa = _attn_call(lambda_q1[l][None, :], lambda_k1[l][None, :],
                        lambda_q2[l][None, :], lambda_k2[l][None, :],
                        subln_g[l][None, :], q, k, vt, sa, kmax, lam_init)
        x = _out_call(oa, c, g, x, wa_bf, wb_bf, wo_bf,
                      ln_g[l][None, :], ln_b[l][None, :], alpha)
    return x
```

```python
import functools
import itertools
import math

import jax
import jax.numpy as jnp
import numpy as np
from jax import lax
from jax.experimental import pallas as pl
from jax.experimental.pallas import tpu as pltpu

CHUNK = 64
HEAD_DIM = 64
V_DIM = 2 * HEAD_DIM
ROPE_THETA = 10000.0
LN_EPS = 1e-5
RMS_EPS = 1e-5

V7X_LANES = 128
V7X_SUBLANES = 8
V7X_VMEM_LIMIT_BYTES = 56 * 1024 * 1024

PROJ_ROWS = 512
PROJ_COLS = 256
PROJ_CAST_COLS = 512
ATTN_ROWS = 256
ATTN_KEYS = 2048
ATTN_HEADS_PER_STEP = 2
SOFTMAX_HEADROOM = 16.0
SOFTMAX_L_MIN = 2.0 ** -80
SOFTMAX_L_MAX = 2.0 ** 60
OUT_ROWS = 1024
OUT_SUB_ROWS = 256
OUT_COLS = 256

_NT_DIMS = (((1,), (1,)), ((), ()))


def _sigmoid(z):
    return 1.0 / (1.0 + jnp.exp(-z))


def _run_pipelined(stages):
    pending = None
    for dots, epilogue in stages:
        acc = dots()
        if pending is not None:
            pending[0](*pending[1])
        pending = (epilogue, acc)
    pending[0](*pending[1])


def _proj_kernel(x_ref, w_hbm, cos_ref, sin_ref, bg_ref, cw_ref, cb_ref,
                 wa_f32_ref, wb_f32_ref, wo_f32_ref,
                 q_ref, k_ref, vt_ref, sa_ref, c_ref, g_ref, kmax_ref,
                 wa_bf_ref, wb_bf_ref, wo_bf_ref,
                 ubuf, w_ref, wvt_ref, stage, sem, *, d_model, layer):
    tm = x_ref.shape[0]
    d = d_model
    cw = PROJ_COLS

    for src, dst in ((wa_f32_ref, wa_bf_ref), (wb_f32_ref, wb_bf_ref), (wo_f32_ref, wo_bf_ref)):
        dst[...] = src[...].astype(dst.dtype)

    @pl.when(jnp.logical_and(pl.program_id(0) == 0, pl.program_id(1) == 0))
    def _():
        cc = stage.shape[2]
        n_chunks = w_ref.shape[1] // cc

        def chunk_copy(c):
            return pltpu.make_async_copy(w_hbm.at[layer, :, pl.ds(c * cc, cc)],
                                         stage.at[c % 2], sem.at[c % 2])

        chunk_copy(0).start(priority=0)
        for c in range(n_chunks):
            if c + 1 < n_chunks:
                chunk_copy(c + 1).start(priority=(c + 1) % 2)
            chunk_copy(c).wait()
            lo = c * cc
            if 2 * d <= lo < 3 * d:
                wvt_ref[lo - 2 * d:lo - 2 * d + cc, :] = stage[c % 2].T.astype(wvt_ref.dtype)
            else:
                w_ref[:, lo:lo + cc] = stage[c % 2].astype(w_ref.dtype)

    xb = x_ref[...].astype(jnp.bfloat16)

    def proj(lo):
        return jnp.dot(xb, w_ref[:, lo:lo + cw], preferred_element_type=jnp.float32)

    cos = cos_ref[...]
    sin_signed = sin_ref[...]
    lane = lax.broadcasted_iota(jnp.int32, (tm, V7X_LANES), 1)
    first_half = (lane & (HEAD_DIM - 1)) < (HEAD_DIM // 2)

    def rotary(t):
        partner = jnp.where(first_half,
                            pltpu.roll(t, V7X_LANES - HEAD_DIM // 2, 1),
                            pltpu.roll(t, HEAD_DIM // 2, 1))
        return t * cos + partner * sin_signed

    scale = HEAD_DIM ** -0.5 * math.log2(math.e)

    def rotary_epilogue(out_ref, lo, mult, absmax_ref=None):
        def epilogue(t):
            for j in range(0, cw, V7X_LANES):
                sl = slice(lo + j, lo + j + V7X_LANES)
                r = rotary(t[:, j:j + V7X_LANES])
                if mult != 1.0:
                    r = r * mult
                out_ref[:, sl] = r.astype(out_ref.dtype)
                if absmax_ref is not None:
                    absmax_ref[:, sl] = jnp.maximum(
                        absmax_ref[:, sl], jnp.max(jnp.abs(r), axis=0, keepdims=True))
        return epilogue

    def vt_stage(lo):
        def dots():
            return [lax.dot_general(wvt_ref[lo:lo + cw, :], xb, _NT_DIMS,
                                    preferred_element_type=jnp.float32)]

        def epilogue(t):
            vt_ref[lo:lo + cw, :] = t.astype(vt_ref.dtype)
        return dots, epilogue

    def silu_epilogue(lo):
        def epilogue(za):
            sa_ref[:, lo:lo + cw] = (za * _sigmoid(za)).astype(sa_ref.dtype)
        return epilogue

    def conv_in_epilogue(lo):
        def epilogue(h, cgate):
            ubuf[V7X_SUBLANES:V7X_SUBLANES + tm, lo:lo + cw] = cgate * h
        return epilogue

    def conv_out_epilogue(lo):
        sl = slice(lo, lo + cw)

        def epilogue(bgate, zb):
            u = ubuf[V7X_SUBLANES:V7X_SUBLANES + tm, sl]
            u1 = ubuf[V7X_SUBLANES - 1:V7X_SUBLANES - 1 + tm, sl]
            u2 = ubuf[V7X_SUBLANES - 2:V7X_SUBLANES - 2 + tm, sl]
            conv = (cw_ref[0:1, sl] * u2 + cw_ref[1:2, sl] * u1
                    + cw_ref[2:3, sl] * u + cb_ref[:, sl])
            c_ref[:, sl] = (bgate * conv * (zb * _sigmoid(zb))).astype(c_ref.dtype)
            ubuf[0:V7X_SUBLANES, sl] = ubuf[tm:tm + V7X_SUBLANES, sl]
        return epilogue

    def gate_epilogue(lo):
        def epilogue(gl):
            g_ref[:, lo:lo + cw] = _sigmoid(gl + bg_ref[:, lo:lo + cw]).astype(g_ref.dtype)
        return epilogue

    chunks = range(0, d, cw)
    def proj_stage(cols, epilogue):
        return (lambda: [proj(lo) for lo in cols]), epilogue

    heavy = ([proj_stage((3 * d + lo,), silu_epilogue(lo)) for lo in chunks]
             + [proj_stage((8 * d + lo,), gate_epilogue(lo)) for lo in range(0, 2 * d, cw)]
             + [proj_stage((5 * d + lo, 7 * d + lo), conv_out_epilogue(lo)) for lo in chunks])
    light = [proj_stage((4 * d + lo, 6 * d + lo), conv_in_epilogue(lo)) for lo in chunks]
    for lo in chunks:
        light.append(proj_stage((lo,), rotary_epilogue(q_ref, lo, scale)))
        light.append(proj_stage((d + lo,), rotary_epilogue(k_ref, lo, 1.0, kmax_ref)))
    light += [vt_stage(lo) for lo in chunks]
    assert len(heavy) == len(light)
    stages = [stage for pair in zip(heavy, light) for stage in pair]

    @pl.when(pl.program_id(1) == 0)
    def _():
        ubuf[0:V7X_SUBLANES, :] = jnp.zeros((V7X_SUBLANES, d), jnp.float32)
        kmax_ref[...] = jnp.zeros_like(kmax_ref)

    _run_pipelined(stages)


def _proj_call(x, w_in, cos, sin_signed, b_gate, conv_w, conv_b, out_weights, layer):
    bsz, seq, d = x.shape
    in_width = w_in.shape[2]
    assert in_width % PROJ_CAST_COLS == 0
    tm = PROJ_ROWS
    n_i = seq // tm
    slab = d // (bsz * n_i)
    assert slab * bsz * n_i == d and slab % (2 * V7X_SUBLANES) == 0
    slab_in = pl.BlockSpec((None, slab, d), lambda b, i: (layer, b * n_i + i, 0))
    slab_out = pl.BlockSpec((slab, d), lambda b, i: (b * n_i + i, 0))
    w_bf = jax.ShapeDtypeStruct((d, d), jnp.bfloat16)
    act = jax.ShapeDtypeStruct((bsz, seq, d), jnp.bfloat16)
    row_spec = pl.BlockSpec((None, tm, d), lambda b, i: (b, i, 0))
    tab_spec = pl.BlockSpec((tm, V7X_LANES), lambda b, i: (i, 0))

    def full(shape):
        return pl.BlockSpec(shape, lambda b, i: (0,) * len(shape))

    return pl.pallas_call(
        functools.partial(_proj_kernel, d_model=d, layer=layer),
        grid=(bsz, seq // tm),
        in_specs=[
            row_spec,
            pl.BlockSpec(memory_space=pl.ANY),
            tab_spec, tab_spec,
            full(b_gate.shape), full(conv_w.shape), full(conv_b.shape),
            slab_in, slab_in, slab_in,
        ],
        out_specs=[row_spec, row_spec,
                   pl.BlockSpec((None, d, tm), lambda b, i: (b, 0, i)),
                   row_spec, row_spec,
                   pl.BlockSpec((None, tm, 2 * d), lambda b, i: (b, i, 0)),
                   pl.BlockSpec((None, 1, d), lambda b, i: (b, 0, 0)),
                   slab_out, slab_out, slab_out],
        out_shape=[act, act, jax.ShapeDtypeStruct((bsz, d, seq), jnp.bfloat16), act, act,
                   jax.ShapeDtypeStruct((bsz, seq, 2 * d), jnp.bfloat16),
                   jax.ShapeDtypeStruct((bsz, 1, d), jnp.float32), w_bf, w_bf, w_bf],
        scratch_shapes=[pltpu.VMEM((tm + V7X_SUBLANES, d), jnp.float32),
                        pltpu.VMEM((d, in_width), jnp.bfloat16),
                        pltpu.VMEM((d, d), jnp.bfloat16),
                        pltpu.VMEM((2, d, PROJ_CAST_COLS), jnp.float32),
                        pltpu.SemaphoreType.DMA((2,))],
        compiler_params=pltpu.CompilerParams(
            dimension_semantics=("arbitrary", "arbitrary"),
            vmem_limit_bytes=V7X_VMEM_LIMIT_BYTES),
        name="proj_rotary_conv_gates",
    )(x, w_in, cos, sin_signed, b_gate, conv_w, conv_b, *out_weights)


def _attn_kernel(lq1_ref, lk1_ref, lq2_ref, lk2_ref, gain_ref,
                 q_ref, k_ref, vt_ref, sa_ref, kmax_ref, o_ref, *, lam_init):
    seq = q_ref.shape[0]
    tq = ATTN_ROWS
    bk = ATTN_KEYS
    f32 = jnp.float32
    lam = (jnp.exp(jnp.sum(lq1_ref[...] * lk1_ref[...], axis=-1, keepdims=True))
           - jnp.exp(jnp.sum(lq2_ref[...] * lk2_ref[...], axis=-1, keepdims=True))
           + lam_init)

    lane = lax.broadcasted_iota(jnp.int32, (tq, V_DIM), 1)
    map1 = lane < HEAD_DIM
    key = lax.broadcasted_iota(jnp.int32, (tq, 2 * tq), 0)
    col = lax.broadcasted_iota(jnp.int32, (tq, 2 * tq), 1)
    qpos = jnp.where(col >= tq, col - tq, col)
    allowed = (key // CHUNK) <= (qpos // CHUNK)
    gain = gain_ref[...] * (1.0 - lam_init)

    def key_blocks(i):
        blocks = [(lo, min(bk, i * tq - lo), False) for lo in range(0, i * tq, bk)]
        return blocks + [(i * tq, tq, True)]

    def head_lanes(hh):
        return slice(hh * V_DIM, (hh + 1) * V_DIM)

    class Tile:
        def __init__(self, i, hh):
            self.i = i
            self.lanes = head_lanes(hh)
            self.blocks = key_blocks(i)
            self.s, self.e = [], []
            self.m = self.l = self.acc = self.ratio = None

    def stacked_q(t):
        q = q_ref[t.i * tq:(t.i + 1) * tq, t.lanes]
        zero = jnp.zeros_like(q)
        return jnp.concatenate([jnp.where(map1, q, zero), jnp.where(map1, zero, q)], axis=0)

    def score_block(t, q2, blk):
        lo, size, masked = blk
        s = lax.dot_general(k_ref[lo:lo + size, t.lanes], q2, _NT_DIMS,
                            preferred_element_type=f32)
        if masked:
            s = jnp.where(allowed, s, -jnp.inf)
        t.s.append(s)
        bm = jnp.max(s, axis=0, keepdims=True)
        t.m = bm if t.m is None else jnp.maximum(t.m, bm)

    def exp_block(t, j):
        e = jnp.exp2(t.s[j] - t.m)
        bl = jnp.sum(e, axis=0, keepdims=True)
        t.l = bl if t.l is None else t.l + bl
        t.e.append(e.astype(jnp.bfloat16))

    def pv_block(t, j):
        lo, size, _ = t.blocks[j]
        if t.ratio is None:
            t.ratio = (lam * t.l[:, :tq] / t.l[:, tq:]).astype(jnp.bfloat16)
        e = t.e[j]
        a = e[:, :tq] - t.ratio * e[:, tq:]
        pv = jnp.dot(vt_ref[t.lanes, lo:lo + size], a, preferred_element_type=f32)
        t.acc = pv if t.acc is None else t.acc + pv

    def finish(t):
        rows = slice(t.i * tq, (t.i + 1) * tq)
        o = t.acc / t.l[:, :tq]
        ms = jnp.mean(o * o, axis=0, keepdims=True)
        y = (o * lax.rsqrt(ms + RMS_EPS)).T * gain
        o_ref[rows, t.lanes] = (y * sa_ref[rows, t.lanes].astype(f32)).astype(o_ref.dtype)

    def interleave(*work):
        for group in itertools.zip_longest(*work):
            for fn in group:
                if fn is not None:
                    fn()

    n_tiles = seq // tq
    n_heads = q_ref.shape[1] // V_DIM

    kabs_max = kmax_ref[...].astype(jnp.bfloat16).astype(f32)
    lane_q = lax.broadcasted_iota(jnp.int32, (2 * tq, V7X_LANES), 1)

    @functools.cache
    def ones_lane0(size):
        lane_k = lax.broadcasted_iota(jnp.int32, (size, V7X_LANES), 1)
        return jnp.where(lane_k == 0, 1.0, 0.0).astype(jnp.bfloat16)

    def shifted_q(t):
        q2 = stacked_q(t)
        bound = jnp.sum(jnp.abs(q2.astype(f32)) * kabs_max[:, t.lanes], axis=-1, keepdims=True)
        neg_shift = jnp.where(lane_q == 0, SOFTMAX_HEADROOM - bound, 0.0)
        return jnp.concatenate([q2, neg_shift.astype(jnp.bfloat16)], axis=1)

    def shifted_exp_block(t, q2a, blk):
        lo, size, masked = blk
        k_aug = jnp.concatenate([k_ref[lo:lo + size, t.lanes], ones_lane0(size)], axis=1)
        s = lax.dot_general(k_aug, q2a, _NT_DIMS, preferred_element_type=f32)
        if masked:
            s = jnp.where(allowed, s, -jnp.inf)
        e = jnp.exp2(s)
        bl = jnp.sum(e, axis=0, keepdims=True)
        t.l = bl if t.l is None else t.l + bl
        t.e.append(e.astype(jnp.bfloat16))

    order = list(range(1, n_tiles, 2)) + list(reversed(range(0, n_tiles, 2)))
    tiles = [Tile(i, hh) for hh in range(n_heads) for i in order]
    for step in range(len(tiles) + 1):
        work = []
        if step < len(tiles):
            t = tiles[step]
            q2a = shifted_q(t)
            work.append([functools.partial(shifted_exp_block, t, q2a, blk) for blk in t.blocks])
        else:
            l_min = functools.reduce(jnp.minimum, [t.l for t in tiles])
            l_max = functools.reduce(jnp.maximum, [t.l for t in tiles])
            in_range = jnp.logical_and(jnp.min(l_min) >= SOFTMAX_L_MIN,
                                       jnp.max(l_max) <= SOFTMAX_L_MAX)
        if step >= 1:
            t = tiles[step - 1]
            work.append([functools.partial(pv_block, t, j) for j in range(len(t.blocks))]
                        + [functools.partial(finish, t)])
        interleave(*work)

    @pl.when(jnp.logical_not(in_range))
    def _():
        for hh, i in itertools.product(range(n_heads), range(n_tiles)):
            t = Tile(i, hh)
            q2 = stacked_q(t)
            for blk in t.blocks:
                score_block(t, q2, blk)
            for j in range(len(t.blocks)):
                exp_block(t, j)
            for j in range(len(t.blocks)):
                pv_block(t, j)
            finish(t)


def _attn_call(lq1, lk1, lq2, lk2, gain, q, k, vt, sa, kmax, lam_init):
    bsz, seq, width = q.shape
    n_heads = width // V_DIM
    hw = ATTN_HEADS_PER_STEP * V_DIM
    head_spec = pl.BlockSpec((None, seq, hw), lambda b, h: (b, 0, h))
    vt_spec = pl.BlockSpec((None, hw, seq), lambda b, h: (b, h, 0))
    kmax_spec = pl.BlockSpec((None, 1, hw), lambda b, h: (b, 0, h))

    def full(shape):
        return pl.BlockSpec(shape, lambda b, h: (0,) * len(shape))

    return pl.pallas_call(
        functools.partial(_attn_kernel, lam_init=lam_init),
        grid=(bsz, n_heads // ATTN_HEADS_PER_STEP),
        in_specs=[full(lq1.shape), full(lk1.shape), full(lq2.shape), full(lk2.shape),
                  full(gain.shape), head_spec, head_spec, vt_spec, head_spec, kmax_spec],
        out_specs=head_spec,
        out_shape=jax.ShapeDtypeStruct((bsz, seq, width), jnp.bfloat16),
        compiler_params=pltpu.CompilerParams(
            dimension_semantics=("arbitrary", "arbitrary"),
            vmem_limit_bytes=V7X_VMEM_LIMIT_BYTES),
        name="diff_attention",
    )(lq1, lk1, lq2, lk2, gain, q, k, vt, sa, kmax)


def _out_kernel(oa_ref, c_ref, g_ref, x_ref, wa_ref, wb_ref, wo_ref,
                lng_ref, lnb_ref, out_ref, *, alpha):
    tm, d = x_ref.shape
    cw = OUT_COLS
    f32 = jnp.float32

    def merge_stage(rows, lo, merged):
        def dots():
            return [jnp.dot(oa_ref[rows, :], wa_ref[:, lo:lo + cw], preferred_element_type=f32),
                    jnp.dot(c_ref[rows, :], wb_ref[:, lo:lo + cw], preferred_element_type=f32)]

        def epilogue(ya, yb):
            merged.append((g_ref[rows, lo:lo + cw].astype(f32) * ya
                           + g_ref[rows, d + lo:d + lo + cw].astype(f32) * yb
                           ).astype(jnp.bfloat16))
        return dots, epilogue

    def out_stage(rows, lo, merged, resid):
        def dots():
            lhs = jnp.concatenate(merged, axis=1)
            return [jnp.dot(lhs, wo_ref[:, lo:lo + cw], preferred_element_type=f32)]

        def epilogue(out):
            resid.append(alpha * x_ref[rows, lo:lo + cw] + out)
            if lo + cw == d:
                mu = sum(jnp.sum(r, axis=-1, keepdims=True) for r in resid) / d
                dev = [r - mu for r in resid]
                var = sum(jnp.sum(v * v, axis=-1, keepdims=True) for v in dev) / d
                inv = lax.rsqrt(var + LN_EPS)
                for j, v in enumerate(dev):
                    cols = slice(j * cw, (j + 1) * cw)
                    out_ref[rows, cols] = v * inv * lng_ref[:, cols] + lnb_ref[:, cols]
        return dots, epilogue

    col_starts = range(0, d, cw)
    sub_tiles = []
    for r0 in range(0, tm, OUT_SUB_ROWS):
        rows = slice(r0, r0 + OUT_SUB_ROWS)
        merged, resid = [], []
        sub_tiles.append(([merge_stage(rows, lo, merged) for lo in col_starts],
                          [out_stage(rows, lo, merged, resid) for lo in col_starts]))
    assert len(sub_tiles) >= 2
    stages = list(sub_tiles[0][0])
    for prev, cur in zip(sub_tiles[:-1], sub_tiles[1:]):
        for m_stage, o_stage in zip(cur[0], prev[1]):
            stages += [m_stage, o_stage]
    stages += sub_tiles[-1][1]
    _run_pipelined(stages)


def _out_call(oa, c, g, x, wa, wb, wo, ln_g, ln_b, alpha):
    bsz, seq, d = x.shape
    tm = OUT_ROWS
    row_spec = pl.BlockSpec((None, tm, d), lambda b, i: (b, i, 0))
    weight_spec = pl.BlockSpec((d, d), lambda b, i: (0, 0), pipeline_mode=pl.Buffered(1))

    def full(shape):
        return pl.BlockSpec(shape, lambda b, i: (0,) * len(shape))

    return pl.pallas_call(
        functools.partial(_out_kernel, alpha=alpha),
        grid=(bsz, seq // tm),
        in_specs=[row_spec, row_spec,
                  pl.BlockSpec((None, tm, 2 * d), lambda b, i: (b, i, 0)),
                  row_spec, weight_spec, weight_spec, weight_spec,
                  full(ln_g.shape), full(ln_b.shape)],
        out_specs=row_spec,
        out_shape=jax.ShapeDtypeStruct((bsz, seq, d), jnp.float32),
        compiler_params=pltpu.CompilerParams(
            dimension_semantics=("arbitrary", "arbitrary"),
            vmem_limit_bytes=V7X_VMEM_LIMIT_BYTES),
        name="merge_out_layernorm",
    )(oa, c, g, x, wa, wb, wo, ln_g, ln_b)


def _rotary_tables(seq_len):
    half = HEAD_DIM // 2
    inv_freq = 1.0 / (ROPE_THETA ** (np.arange(half, dtype=np.float64) / half))
    ang = np.arange(seq_len, dtype=np.float64)[:, None] * inv_freq[None, :]
    cos = np.cos(ang)
    sin = np.sin(ang)
    reps = V7X_LANES // HEAD_DIM
    cos_t = np.tile(np.concatenate([cos, cos], -1), (1, reps))
    sin_t = np.tile(np.concatenate([-sin, sin], -1), (1, reps))
    return jnp.asarray(cos_t, jnp.float32), jnp.asarray(sin_t, jnp.float32)


def kernel(x, w_in, b_gate, lambda_q1, lambda_k1, lambda_q2, lambda_k2, subln_g,
           conv_w, conv_b, w_a_out, w_b_out, w_o, ln_g, ln_b):
    depth = w_in.shape[0]
    seq_len = x.shape[1]
    alpha = (2.0 * depth) ** 0.25
    cos_t, sin_t = _rotary_tables(seq_len)
    for l in range(depth):
        lam_init = 0.8 - 0.6 * math.exp(-0.3 * l)
        q, k, vt, sa, c, g, kmax, wa_bf, wb_bf, wo_bf = _proj_call(
            x, w_in, cos_t, sin_t,
            b_gate[l][None, :], conv_w[l], conv_b[l][None, :], (w_a_out, w_b_out, w_o), l)
        oa = _attn_call(lambda_q1[l][None, :], lambda_k1[l][None, :],
                        lambda_q2[l][None, :], lambda_k2[l][None, :],
                        subln_g[l][None, :], q, k, vt, sa, kmax, lam_init)
        x = _out_call(oa, c, g, x, wa_bf, wb_bf, wo_bf,
                      ln_g[l][None, :], ln_b[l][None, :], alpha)
    return x
```
